```python
import numpy as np
import jax
import jax.numpy as jnp
from jax import lax

D_MODEL = 1024
BATCH = 16
SEQ = 2048
DEPTH = 2

GRID_W = 64
CTX_LEN = 256

NA_HEADS = 8
NA_HEAD_DIM = 64
NA_WIDTH = NA_HEADS * NA_HEAD_DIM
NA_KH = 8
NA_KW = 16
NA_QB = 16
NA_BAND = 32
NA_NCB = GRID_W // NA_QB
LRU_WIDTH = 512
LRU_BLOCKS = 8
LRU_BLOCK = LRU_WIDTH // LRU_BLOCKS
LRU_CONV = 4
LRU_CONV_LEFT = 2
LRU_C = 8.0
SC_WIDTH = D_MODEL
SC_K = 3
D_FF = 2816
FFN_RES = 0.5
N_SUB = 3
N_MOD = 3 * N_SUB
ALPHA = (2.0 * DEPTH) ** 0.25
BETA = (8.0 * DEPTH) ** -0.25
LN_EPS = 1e-5
NEG_INF = -1e30

N_EVEN = (DEPTH + 1) // 2
N_ODD = DEPTH // 2
MIX0_IN = 3 * NA_WIDTH + 2 * LRU_WIDTH
MIX0_OUT = NA_WIDTH + LRU_WIDTH

kernel_name = 'hybrid_na_rglru_shortconv_dit'


def layer_norm(h, g, b):
    hf = h.astype(jnp.float32)
    mu = jnp.mean(hf, axis=-1, keepdims=True)
    var = jnp.mean(jnp.square(hf - mu), axis=-1, keepdims=True)
    return ((hf - mu) * lax.rsqrt(var + LN_EPS)).astype(h.dtype) * g + b


def modulate(h, m, k):
    return h * (1 + m[..., 3 * k + 1, :]) + m[..., 3 * k, :]


def post_norm_residual(h, y, m, k, g, b, res_w):
    return layer_norm(ALPHA * h + res_w * m[..., 3 * k + 2, :] * y, g, b)


def swiglu(h, w1, w3, w2):
    return (jax.nn.silu(h @ w1) * (h @ w3)) @ w2


def depthwise_conv(h, w, left):
    k, ch = w.shape
    return lax.conv_general_dilated(
        h, w[:, None, :].astype(h.dtype), (1,), [(left, k - 1 - left)],
        dimension_numbers=('NWC', 'WIO', 'NWC'), feature_group_count=ch)


def _na_column_tables():
    j = np.arange(NA_NCB)[:, None, None]
    u = np.arange(NA_QB)[None, :, None]
    v = np.arange(NA_BAND)[None, None, :]
    band = np.clip(NA_QB * np.arange(NA_NCB) - NA_KW // 2, 0, GRID_W - NA_BAND)
    q_col = NA_QB * j + u
    k_col = band[:, None, None] + v
    start = np.clip(q_col - NA_KW // 2, 0, GRID_W - NA_KW)
    mask = (k_col >= start) & (k_col < start + NA_KW)
    col_idx = np.clip(k_col - q_col + NA_KW - 1, 0, 2 * NA_KW - 2)
    return [int(s) for s in band], mask, col_idx


def neighbourhood_attention(q, k, v, k_ctx, v_ctx, rpb):
    bsz, n, nh, hd = q.shape
    rows = n // GRID_W
    kh = min(NA_KH, rows)
    nk = kh * NA_BAND
    bands, col_mask, col_idx = _na_column_tables()
    mask = np.broadcast_to(col_mask[:, :, None, :], (NA_NCB, NA_QB, kh, NA_BAND)).reshape(NA_NCB, NA_QB, nk)
    scale = hd ** -0.5
    qg = q.reshape(bsz, rows, NA_NCB, NA_QB, nh, hd)
    kg = k.reshape(bsz, rows, GRID_W, nh, hd)
    vg = v.reshape(bsz, rows, GRID_W, nh, hd)
    rpb_f = rpb.astype(jnp.float32)

    def gather_bands(t_rows):
        t = jnp.stack([t_rows[:, :, s:s + NA_BAND] for s in bands], axis=1)
        return t.reshape(bsz, NA_NCB, nk, nh, hd)

    def row_block(r):
        r0 = jnp.clip(r - kh // 2, 0, rows - kh)
        kb = gather_bands(lax.dynamic_slice_in_dim(kg, r0, kh, axis=1))
        vb = gather_bands(lax.dynamic_slice_in_dim(vg, r0, kh, axis=1))
        qr = lax.dynamic_index_in_dim(qg, r, axis=1, keepdims=False)
        s_loc = jnp.einsum('bjqhd,bjkhd->bhjqk', qr, kb, preferred_element_type=jnp.float32) * scale
        s_ctx = jnp.einsum('bjqhd,bchd->bhjqc', qr, k_ctx, preferred_element_type=jnp.float32) * scale
        row_off = r0 + jnp.arange(kh) - r + NA_KH - 1
        bias = rpb_f[:, row_off][:, :, col_idx]
        bias = bias.transpose(0, 2, 3, 1, 4).reshape(nh, NA_NCB, NA_QB, nk)
        s_loc = jnp.where(mask, s_loc + bias, NEG_INF)
        p = jax.nn.softmax(jnp.concatenate([s_loc, s_ctx], axis=-1), axis=-1).astype(v.dtype)
        return (jnp.einsum('bhjqk,bjkhd->bjqhd', p[..., :nk], vb)
                + jnp.einsum('bhjqc,bchd->bjqhd', p[..., nk:], v_ctx))

    out = lax.map(row_block, jnp.arange(rows))
    return jnp.moveaxis(out, 0, 1).reshape(bsz, n, nh * hd)


def context_attention(q, k, v):
    bsz, n, nh, hd = q.shape
    s = jnp.einsum('bqhd,bkhd->bhqk', q, k, preferred_element_type=jnp.float32) * hd ** -0.5
    p = jax.nn.softmax(s, axis=-1).astype(v.dtype)
    return jnp.einsum('bhqk,bkhd->bqhd', p, v).reshape(bsz, n, nh * hd)


def rglru_coeffs(xc, w_a, b_a, w_x, b_x, lam):
    bsz, n, rw = xc.shape
    xb = xc.reshape(bsz, n, LRU_BLOCKS, LRU_BLOCK)
    gate_r = jax.nn.sigmoid((jnp.einsum('bsnk,nkj->bsnj', xb, w_a).reshape(bsz, n, rw) + b_a).astype(jnp.float32))
    gate_i = jax.nn.sigmoid((jnp.einsum('bsnk,nkj->bsnj', xb, w_x).reshape(bsz, n, rw) + b_x).astype(jnp.float32))
    log_a = LRU_C * gate_r * jax.nn.log_sigmoid(lam.astype(jnp.float32))
    a = jnp.exp(log_a)
    b = jnp.sqrt(-jnp.expm1(2.0 * log_a)) * (gate_i * xc.astype(jnp.float32))
    return a, b


def linear_scan(a, b, h0):
    def combine(left, right):
        return left[0] * right[0], right[0] * left[1] + right[1]
    a_cum, h = lax.associative_scan(combine, (a, b), axis=1)
    return h + a_cum * h0[:, None, :]


def rglru_bidirectional(xc, w_a, b_a, w_x, b_x, lam, h0_fwd, h0_bwd):
    a_f, b_f = rglru_coeffs(xc, w_a[0], b_a[0], w_x[0], b_x[0], lam[0])
    a_b, b_b = rglru_coeffs(xc, w_a[1], b_a[1], w_x[1], b_x[1], lam[1])
    h_f = linear_scan(a_f, b_f, h0_fwd)
    h_b = linear_scan(a_b[:, ::-1], b_b[:, ::-1], h0_bwd)[:, ::-1]
    return (h_f + h_b).astype(xc.dtype), h_f[:, -1], h_b[:, 0]


def even_mixer(z, z_ctx, w_in, rpb, conv_w, conv_b, w_a, b_a, w_x, b_x, lam, w_out, with_ctx_out):
    cuts = [NA_WIDTH, 2 * NA_WIDTH, 3 * NA_WIDTH, 3 * NA_WIDTH + LRU_WIDTH]

    def project(t):
        bsz, n, _ = t.shape
        q, k, v, xr, gr = jnp.split(t @ w_in, cuts, axis=-1)
        hs = (bsz, n, NA_HEADS, NA_HEAD_DIM)
        xr = depthwise_conv(xr, conv_w, LRU_CONV_LEFT) + conv_b
        return q.reshape(hs), k.reshape(hs), v.reshape(hs), xr, gr

    qc, kc, vc, xc, gc = project(z_ctx)
    q, k, v, xl, gl = project(z)
    zeros = jnp.zeros((z.shape[0], LRU_WIDTH), jnp.float32)
    y_c, hf_ctx, hb_ctx = rglru_bidirectional(xc, w_a, b_a, w_x, b_x, lam, zeros, zeros)
    y_l, _, _ = rglru_bidirectional(xl, w_a, b_a, w_x, b_x, lam, hf_ctx, hb_ctx)
    na = neighbourhood_attention(q, k, v, kc, vc, rpb)
    out = jnp.concatenate([na, y_l * jax.nn.gelu(gl)], axis=-1) @ w_out
    out_ctx = None
    if with_ctx_out:
        ca = context_attention(qc, kc, vc)
        out_ctx = jnp.concatenate([ca, y_c * jax.nn.gelu(gc)], axis=-1) @ w_out
    return out, out_ctx


def short_conv_mixer(z, w_in, conv_w, w_out):
    gate_b, gate_c, xv = jnp.split(z @ w_in, 3, axis=-1)
    return (gate_b * depthwise_conv(gate_c * xv, conv_w, 1)) @ w_out


def setup_inputs(seed: int = 0) -> dict:
    key = jax.random.key(seed)
    ks = jax.random.split(key, 24)
    f32 = jnp.float32
    D = D_MODEL

    def nrm(k, shape, s):
        return jax.random.normal(k, shape, f32) * s

    u = jax.random.uniform(ks[19], (N_EVEN, 2, LRU_WIDTH), f32, 0.9, 0.999)
    s = u ** (1.0 / LRU_C)
    return {
        'x': nrm(ks[0], (BATCH, SEQ, D), 1.0),
        'c': nrm(ks[1], (BATCH, D), 1.0),
        'ctx': nrm(ks[2], (BATCH, CTX_LEN, D), 1.0),
        'c_ctx': nrm(ks[3], (D,), 1.0),
        'mod_w': nrm(ks[4], (DEPTH, D, N_MOD * D), 0.5 * D ** -0.5),
        'mod_b': nrm(ks[5], (DEPTH, N_MOD * D), 0.02),
        'ln_g': 1.0 + nrm(ks[6], (DEPTH, N_SUB, D), 0.02),
        'ln_b': nrm(ks[7], (DEPTH, N_SUB, D), 0.02),
        'ffn_w1': nrm(ks[8], (DEPTH, 2, D, D_FF), D ** -0.5),
        'ffn_w3': nrm(ks[9], (DEPTH, 2, D, D_FF), D ** -0.5),
        'ffn_w2': nrm(ks[10], (DEPTH, 2, D_FF, D), BETA * D_FF ** -0.5),
        'mix0_w_in': nrm(ks[11], (N_EVEN, D, MIX0_IN), D ** -0.5),
        'na_rpb': nrm(ks[12], (N_EVEN, NA_HEADS, 2 * NA_KH - 1, 2 * NA_KW - 1), 0.1),
        'lru_conv_w': nrm(ks[13], (N_EVEN, LRU_CONV, LRU_WIDTH), LRU_CONV ** -0.5),
        'lru_conv_b': nrm(ks[14], (N_EVEN, LRU_WIDTH), 0.02),
        'lru_w_a': nrm(ks[15], (N_EVEN, 2, LRU_BLOCKS, LRU_BLOCK, LRU_BLOCK), LRU_BLOCK ** -0.5),
        'lru_b_a': nrm(ks[16], (N_EVEN, 2, LRU_WIDTH), 0.02),
        'lru_w_x': nrm(ks[17], (N_EVEN, 2, LRU_BLOCKS, LRU_BLOCK, LRU_BLOCK), LRU_BLOCK ** -0.5),
        'lru_b_x': nrm(ks[18], (N_EVEN, 2, LRU_WIDTH), 0.02),
        'lru_lambda': jnp.log(s) - jnp.log1p(-s),
        'mix0_w_out': nrm(ks[20], (N_EVEN, MIX0_OUT, D), BETA * MIX0_OUT ** -0.5),
        'mix1_w_in': nrm(ks[21], (N_ODD, D, 3 * SC_WIDTH), D ** -0.5),
        'sconv_w': nrm(ks[22], (N_ODD, SC_K, SC_WIDTH), SC_K ** -0.5),
        'mix1_w_out': nrm(ks[23], (N_ODD, SC_WIDTH, D), BETA * SC_WIDTH ** -0.5),
    }


def reference(x, c, ctx, c_ctx, mod_w, mod_b, ln_g, ln_b, ffn_w1, ffn_w3, ffn_w2,
              mix0_w_in, na_rpb, lru_conv_w, lru_conv_b, lru_w_a, lru_b_a, lru_w_x, lru_b_x,
              lru_lambda, mix0_w_out, mix1_w_in, sconv_w, mix1_w_out):
    h = x
    hc = ctx
    cond = jax.nn.silu(c)
    cond_ctx = jax.nn.silu(c_ctx)
    for layer in range(DEPTH):
        even = layer % 2 == 0
        ctx_next = layer < DEPTH - 1
        ctx_here = ctx_next or even
        m = (cond @ mod_w[layer] + mod_b[layer]).reshape(-1, 1, N_MOD, D_MODEL)
        mc = (cond_ctx @ mod_w[layer] + mod_b[layer]).reshape(N_MOD, D_MODEL)

        w1, w3, w2 = ffn_w1[layer, 0], ffn_w3[layer, 0], ffn_w2[layer, 0]
        g, b = ln_g[layer, 0], ln_b[layer, 0]
        h = post_norm_residual(h, swiglu(modulate(h, m, 0), w1, w3, w2), m, 0, g, b, FFN_RES)
        if ctx_here:
            hc = post_norm_residual(hc, swiglu(modulate(hc, mc, 0), w1, w3, w2), mc, 0, g, b, FFN_RES)

        if even:
            e = layer // 2
            y, yc = even_mixer(modulate(h, m, 1), modulate(hc, mc, 1), mix0_w_in[e], na_rpb[e],
                               lru_conv_w[e], lru_conv_b[e], lru_w_a[e], lru_b_a[e], lru_w_x[e],
                               lru_b_x[e], lru_lambda[e], mix0_w_out[e], ctx_next)
        else:
            o = layer // 2
            y = short_conv_mixer(modulate(h, m, 1), mix1_w_in[o], sconv_w[o], mix1_w_out[o])
            yc = short_conv_mixer(modulate(hc, mc, 1), mix1_w_in[o], sconv_w[o], mix1_w_out[o]) if ctx_next else None
        g, b = ln_g[layer, 1], ln_b[layer, 1]
        h = post_norm_residual(h, y, m, 1, g, b, 1.0)

        w1, w3, w2 = ffn_w1[layer, 1], ffn_w3[layer, 1], ffn_w2[layer, 1]
        g2, b2 = ln_g[layer, 2], ln_b[layer, 2]
        if ctx_next:
            hc = post_norm_residual(hc, yc, mc, 1, g, b, 1.0)
            hc = post_norm_residual(hc, swiglu(modulate(hc, mc, 2), w1, w3, w2), mc, 2, g2, b2, FFN_RES)
        h = post_norm_residual(h, swiglu(modulate(h, m, 2), w1, w3, w2), m, 2, g2, b2, FFN_RES)
    return h
```

```python
import functools

import numpy as np
import jax
import jax.numpy as jnp
from jax import lax
from jax.experimental import pallas as pl
from jax.experimental.pallas import tpu as pltpu

F32 = jnp.float32
BF16 = jnp.bfloat16

D_MODEL = 1024
BATCH = 16
SEQ = 2048
DEPTH = 2
GRID_W = 64
CTX_LEN = 256

NA_HEADS = 8
NA_HEAD_DIM = 64
NA_WIDTH = NA_HEADS * NA_HEAD_DIM
NA_KH = 8
NA_KW = 16
LRU_WIDTH = 512
LRU_BLOCKS = 8
LRU_BLOCK = LRU_WIDTH // LRU_BLOCKS
LRU_CONV = 4
LRU_C = 8.0
SC_WIDTH = D_MODEL
SC_K = 3
D_FF = 2816
FFN_RES = 0.5
N_SUB = 3
N_MOD = 3 * N_SUB
ALPHA = (2.0 * DEPTH) ** 0.25
LN_EPS = 1e-5
NEG_INF = -1e30

ROWS = SEQ // GRID_W

SUBLANES = 8
LANES = 128
VMEM_LIMIT_BYTES = 56 * 1024 * 1024

MOD_ROWS = 24
CTX_ROW = BATCH
TOKEN_TILE = 512
NA_GROUP_ROWS = 4
NA_KEY_ROWS = 12
NA_GROUPS = ROWS // NA_GROUP_ROWS
LRU_CHUNK = 256
HALO = SUBLANES


def _params(n_axes):
    return pltpu.CompilerParams(dimension_semantics=("arbitrary",) * n_axes,
                                vmem_limit_bytes=VMEM_LIMIT_BYTES)


def _resident(shape):
    return pl.BlockSpec(shape, lambda *_: (0,) * len(shape), pipeline_mode=pl.Buffered(1))


def _sigmoid(x):
    return 1.0 / (1.0 + jnp.exp(-x))


def _mod_rows(m_ref, sub):
    shift = m_ref[0, 3 * sub:3 * sub + 1, :]
    scale = m_ref[0, 3 * sub + 1:3 * sub + 2, :]
    gate = m_ref[0, 3 * sub + 2:3 * sub + 3, :]
    return shift, scale, gate


def _post_norm(x, y, gate, res_w, g, b):
    r = ALPHA * x + (res_w * gate) * y
    mu = jnp.mean(r, axis=-1, keepdims=True)
    rc = r - mu
    var = jnp.mean(rc * rc, axis=-1, keepdims=True)
    return rc * lax.rsqrt(var + LN_EPS) * g + b


def _mod_kernel(c_ref, w_ref, b_ref, o_ref):
    cv = c_ref[...]
    cond = (cv * _sigmoid(cv)).astype(BF16)
    o_ref[...] = jnp.dot(cond, w_ref[...].astype(BF16), preferred_element_type=F32) + b_ref[...]


def _modulation(c, c_ctx, mod_w, mod_b):
    pad = jnp.zeros((MOD_ROWS - BATCH - 1, D_MODEL), F32)
    cond = jnp.concatenate([c, c_ctx[None, :], pad], axis=0)
    tn = D_MODEL
    out = pl.pallas_call(
        _mod_kernel,
        out_shape=jax.ShapeDtypeStruct((DEPTH, MOD_ROWS, N_MOD * D_MODEL), F32),
        grid=(DEPTH, N_MOD * D_MODEL // tn),
        in_specs=[
            pl.BlockSpec((MOD_ROWS, D_MODEL), lambda l, j: (0, 0)),
            pl.BlockSpec((None, D_MODEL, tn), lambda l, j: (l, 0, j)),
            pl.BlockSpec((None, 1, tn), lambda l, j: (l, 0, j)),
        ],
        out_specs=pl.BlockSpec((None, MOD_ROWS, tn), lambda l, j: (l, 0, j)),
        compiler_params=_params(2),
        name="modulation",
    )(cond, mod_w, mod_b.reshape(DEPTH, 1, N_MOD * D_MODEL))
    return out.reshape(DEPTH, MOD_ROWS, N_MOD, D_MODEL)


def _ffn_kernel(sub, h_ref, m_ref, w1_ref, w3_ref, w2_ref, g_ref, b_ref, o_ref):
    x = h_ref[...]
    shift, scale, gate = _mod_rows(m_ref, sub)
    z = (x * (1.0 + scale) + shift).astype(BF16)
    a = jnp.dot(z, w1_ref[...], preferred_element_type=F32)
    bb = jnp.dot(z, w3_ref[...], preferred_element_type=F32)
    u = (a * _sigmoid(a) * bb).astype(BF16)
    y = jnp.dot(u, w2_ref[...], preferred_element_type=F32)
    o_ref[...] = _post_norm(x, y, gate, FFN_RES, g_ref[...], b_ref[...])


def _ffn(h, m, row_of_tile, w1, w3, w2, g, b, sub):
    n = h.shape[0]
    tm = TOKEN_TILE
    return pl.pallas_call(
        functools.partial(_ffn_kernel, sub),
        out_shape=jax.ShapeDtypeStruct((n, D_MODEL), F32),
        grid=(n // tm,),
        in_specs=[
            pl.BlockSpec((tm, D_MODEL), lambda i: (i, 0)),
            pl.BlockSpec((1, N_MOD, D_MODEL), lambda i: (row_of_tile(i), 0, 0)),
            _resident((D_MODEL, D_FF)),
            _resident((D_MODEL, D_FF)),
            _resident((D_FF, D_MODEL)),
            _resident((1, D_MODEL)),
            _resident((1, D_MODEL)),
        ],
        out_specs=pl.BlockSpec((tm, D_MODEL), lambda i: (i, 0)),
        compiler_params=_params(1),
        name="ffn",
    )(h, m, w1, w3, w2, g, b)


def _proj_kernel(sub, outs, h_ref, m_ref, w_ref, *o_refs):
    shift, scale, _ = _mod_rows(m_ref, sub)
    z = (h_ref[...] * (1.0 + scale) + shift).astype(BF16)
    p = jnp.dot(z, w_ref[...], preferred_element_type=F32)
    for (start, width, _, mult), o_ref in zip(outs, o_refs):
        v = p[:, start:start + width]
        if mult != 1.0:
            v = v * mult
        o_ref[...] = v.astype(o_ref.dtype)


def _proj(h, m, row_of_tile, w, outs, sub):
    n = h.shape[0]
    tm = TOKEN_TILE
    return pl.pallas_call(
        functools.partial(_proj_kernel, sub, outs),
        out_shape=[jax.ShapeDtypeStruct((n, width), dt) for _, width, dt, _ in outs],
        grid=(n // tm,),
        in_specs=[
            pl.BlockSpec((tm, D_MODEL), lambda i: (i, 0)),
            pl.BlockSpec((1, N_MOD, D_MODEL), lambda i: (row_of_tile(i), 0, 0)),
            _resident(w.shape),
        ],
        out_specs=[pl.BlockSpec((tm, width), lambda i: (i, 0)) for _, width, _, _ in outs],
        compiler_params=_params(1),
        name="mix0_in_proj",
    )(h, m, w)


def _row_index(n):
    return lax.broadcasted_iota(jnp.int32, (n, 1), 0)


def _lru_coeffs(x_ref, n, cw_ref, cb_ref, wg_ref, bg_ref, log_sig, xc_s, a_s, b_s):
    x = x_ref[...]
    t = _row_index(n)
    xc = cw_ref[2:3, :] * x + cb_ref[...]
    xc = xc + cw_ref[0:1, :] * jnp.where(t >= 2, pltpu.roll(x, 2, 0), 0.0)
    xc = xc + cw_ref[1:2, :] * jnp.where(t >= 1, pltpu.roll(x, 1, 0), 0.0)
    xc = xc + cw_ref[3:4, :] * jnp.where(t < n - 1, pltpu.roll(x, n - 1, 0), 0.0)
    xc_s[0:n, :] = xc

    tin = _row_index(LRU_CHUNK) % SUBLANES
    for c0 in range(0, n, LRU_CHUNK):
        xcc = xc_s[c0:c0 + LRU_CHUNK, :]
        gts = jnp.dot(xcc.astype(BF16), wg_ref[...], preferred_element_type=F32) + bg_ref[...]
        for d in range(2):
            base = 2 * LRU_WIDTH * d
            gate_r = _sigmoid(gts[:, base:base + LRU_WIDTH])
            gate_i = _sigmoid(gts[:, base + LRU_WIDTH:base + 2 * LRU_WIDTH])
            log_a = (LRU_C * gate_r) * log_sig[d:d + 1, :]
            a = jnp.exp(log_a)
            bb = jnp.sqrt(-jnp.tanh(log_a) * (a * a + 1.0)) * (gate_i * xcc)
            for s in (1, 2, 4):
                if d == 0:
                    ok = tin >= s
                    sh = s
                else:
                    ok = tin < SUBLANES - s
                    sh = LRU_CHUNK - s
                a_n = jnp.where(ok, pltpu.roll(a, sh, 0), 1.0)
                b_n = jnp.where(ok, pltpu.roll(bb, sh, 0), 0.0)
                bb = a * b_n + bb
                a = a * a_n
            a_s[d, c0:c0 + LRU_CHUNK, :] = a
            b_s[d, c0:c0 + LRU_CHUNK, :] = bb


def _lru_carry(n, a_s, b_s, h_fwd, h_bwd):
    nblk = n // SUBLANES

    def body(j, carry):
        hf, hb = carry
        rf = pl.multiple_of(j * SUBLANES, SUBLANES)
        of = a_s[0, pl.ds(rf, SUBLANES), :] * hf + b_s[0, pl.ds(rf, SUBLANES), :]
        b_s[0, pl.ds(rf, SUBLANES), :] = of
        rb = pl.multiple_of((nblk - 1 - j) * SUBLANES, SUBLANES)
        ob = a_s[1, pl.ds(rb, SUBLANES), :] * hb + b_s[1, pl.ds(rb, SUBLANES), :]
        b_s[1, pl.ds(rb, SUBLANES), :] = ob
        return of[SUBLANES - 1:SUBLANES, :], ob[0:1, :]

    return lax.fori_loop(0, nblk, body, (h_fwd, h_bwd), unroll=4)


def _gelu_tanh(x):
    return 0.5 * x * (1.0 + jnp.tanh(0.7978845608028654 * (x + 0.044715 * (x * x * x))))


def _lru_kernel(xc_ref, xl_ref, gl_ref, cw_ref, cb_ref, wg_ref, bg_ref, lam_ref, o_ref,
                xc_s, a_s, b_s):
    lam = lam_ref[...]
    log_sig = jnp.minimum(lam, 0.0) - jnp.log1p(jnp.exp(-jnp.abs(lam)))
    zero = jnp.zeros((1, LRU_WIDTH), F32)
    _lru_coeffs(xc_ref, CTX_LEN, cw_ref, cb_ref, wg_ref, bg_ref, log_sig, xc_s, a_s, b_s)
    hf, hb = _lru_carry(CTX_LEN, a_s, b_s, zero, zero)
    _lru_coeffs(xl_ref, SEQ, cw_ref, cb_ref, wg_ref, bg_ref, log_sig, xc_s, a_s, b_s)
    _lru_carry(SEQ, a_s, b_s, hf, hb)
    for c0 in range(0, SEQ, LRU_CHUNK):
        y = b_s[0, c0:c0 + LRU_CHUNK, :] + b_s[1, c0:c0 + LRU_CHUNK, :]
        o_ref[c0:c0 + LRU_CHUNK, :] = (y * _gelu_tanh(gl_ref[c0:c0 + LRU_CHUNK, :])).astype(BF16)


def _block_diag(w):
    eye = jnp.eye(LRU_BLOCKS, dtype=w.dtype)
    return jnp.einsum('nkj,nm->nkmj', w, eye).reshape(LRU_WIDTH, LRU_WIDTH)


def _lru(xr_ctx, xr_lat, g_lat, conv_w, conv_b, w_a, b_a, w_x, b_x, lam):
    wg = jnp.concatenate([_block_diag(w_a[0]), _block_diag(w_x[0]),
                          _block_diag(w_a[1]), _block_diag(w_x[1])], axis=1).astype(BF16)
    bg = jnp.concatenate([b_a[0], b_x[0], b_a[1], b_x[1]])[None, :]
    return pl.pallas_call(
        _lru_kernel,
        out_shape=jax.ShapeDtypeStruct((BATCH * SEQ, LRU_WIDTH), BF16),
        grid=(BATCH,),
        in_specs=[
            pl.BlockSpec((CTX_LEN, LRU_WIDTH), lambda i: (i, 0)),
            pl.BlockSpec((SEQ, LRU_WIDTH), lambda i: (i, 0)),
            pl.BlockSpec((SEQ, LRU_WIDTH), lambda i: (i, 0)),
            _resident((LRU_CONV, LRU_WIDTH)),
            _resident((1, LRU_WIDTH)),
            _resident((LRU_WIDTH, 4 * LRU_WIDTH)),
            _resident((1, 4 * LRU_WIDTH)),
            _resident((2, LRU_WIDTH)),
        ],
        out_specs=pl.BlockSpec((SEQ, LRU_WIDTH), lambda i: (i, 0)),
        scratch_shapes=[
            pltpu.VMEM((SEQ, LRU_WIDTH), F32),
            pltpu.VMEM((2, SEQ, LRU_WIDTH), F32),
            pltpu.VMEM((2, SEQ, LRU_WIDTH), F32),
        ],
        compiler_params=_params(1),
        name="rglru",
    )(xr_ctx, xr_lat, g_lat, conv_w, conv_b[None, :], wg, bg, lam)


def _na_key_row_start(g):
    return int(np.clip(NA_GROUP_ROWS * g - NA_KH // 2, 0, ROWS - NA_KEY_ROWS))


def _na_variant(g):
    return 0 if g == 0 else (2 if g == NA_GROUPS - 1 else 1)


def _na_tables():
    ridx, cidx, valid = {}, {}, {}
    for g in range(NA_GROUPS):
        ks = _na_key_row_start(g)
        r = (NA_GROUP_ROWS * g + np.arange(NA_GROUP_ROWS))[:, None, None, None]
        cq = np.arange(GRID_W)[None, :, None, None]
        kr = (ks + np.arange(NA_KEY_ROWS))[None, None, :, None]
        kc = np.arange(GRID_W)[None, None, None, :]
        r0 = np.clip(r - NA_KH // 2, 0, ROWS - NA_KH)
        start = np.clip(cq - NA_KW // 2, 0, GRID_W - NA_KW)
        ok = (kr >= r0) & (kr < r0 + NA_KH) & (kc >= start) & (kc < start + NA_KW)
        ri = np.clip(kr - r + NA_KH - 1, 0, 2 * NA_KH - 2)
        ci = np.clip(kc - cq + NA_KW - 1, 0, 2 * NA_KW - 2)
        shape = (NA_GROUP_ROWS * GRID_W, NA_KEY_ROWS * GRID_W)
        full = (NA_GROUP_ROWS, GRID_W, NA_KEY_ROWS, GRID_W)
        tabs = (np.broadcast_to(ri, full).reshape(shape), np.broadcast_to(ci, full).reshape(shape),
                np.broadcast_to(ok, full).reshape(shape))
        v = _na_variant(g)
        if v in ridx:
            assert all(np.array_equal(x, y) for x, y in zip(tabs, (ridx[v], cidx[v], valid[v])))
        else:
            ridx[v], cidx[v], valid[v] = tabs
    order = sorted(ridx)
    return (np.stack([ridx[v] for v in order]), np.stack([cidx[v] for v in order]),
            np.stack([valid[v] for v in order]))


def _na_kernel(q_ref, k_ref, v_ref, kc_ref, vc_ref, bias_ref, o_ref):
    gq = NA_GROUP_ROWS * GRID_W
    gk = NA_KEY_ROWS * GRID_W
    lane = lax.broadcasted_iota(jnp.int32, (1, LANES), 1)
    kc = kc_ref[...]
    vc = vc_ref[...]
    contract_last = (((1,), (1,)), ((), ()))
    for g in range(NA_GROUPS):
        k0 = _na_key_row_start(g) * GRID_W
        q = q_ref[g * gq:(g + 1) * gq, :]
        kw = k_ref[k0:k0 + gk, :]
        vw = v_ref[k0:k0 + gk, :]
        outs = []
        for hh in range(LANES // NA_HEAD_DIM):
            head_lanes = (lane >= hh * NA_HEAD_DIM) & (lane < (hh + 1) * NA_HEAD_DIM)
            qh = jnp.where(head_lanes, q, jnp.zeros_like(q))
            s_loc = lax.dot_general(qh, kw, contract_last, preferred_element_type=F32)
            s_loc = s_loc + bias_ref[hh, _na_variant(g)]
            s_ctx = lax.dot_general(qh, kc, contract_last, preferred_element_type=F32)
            mx = jnp.maximum(jnp.max(s_loc, axis=-1, keepdims=True),
                             jnp.max(s_ctx, axis=-1, keepdims=True))
            p_loc = jnp.exp(s_loc - mx)
            p_ctx = jnp.exp(s_ctx - mx)
            den = jnp.sum(p_loc, axis=-1, keepdims=True) + jnp.sum(p_ctx, axis=-1, keepdims=True)
            o = (jnp.dot(p_loc.astype(BF16), vw, preferred_element_type=F32)
                 + jnp.dot(p_ctx.astype(BF16), vc, preferred_element_type=F32))
            outs.append((head_lanes, o / den))
        res = outs[0][1]
        for head_lanes, o in outs[1:]:
            res = jnp.where(head_lanes, o, res)
        o_ref[g * gq:(g + 1) * gq, :] = res.astype(BF16)


def _neighbourhood_attention(q, k, v, kc, vc, rpb):
    ridx, cidx, valid = _na_tables()
    bias = jnp.where(valid[None], rpb.astype(F32)[:, ridx, cidx], NEG_INF)
    heads_per_step = LANES // NA_HEAD_DIM
    n_pairs = NA_HEADS // heads_per_step
    gq = NA_GROUP_ROWS * GRID_W
    gk = NA_KEY_ROWS * GRID_W
    return pl.pallas_call(
        _na_kernel,
        out_shape=jax.ShapeDtypeStruct((BATCH * SEQ, NA_WIDTH), BF16),
        grid=(n_pairs, BATCH),
        in_specs=[
            pl.BlockSpec((SEQ, LANES), lambda p, b: (b, p)),
            pl.BlockSpec((SEQ, LANES), lambda p, b: (b, p)),
            pl.BlockSpec((SEQ, LANES), lambda p, b: (b, p)),
            pl.BlockSpec((CTX_LEN, LANES), lambda p, b: (b, p)),
            pl.BlockSpec((CTX_LEN, LANES), lambda p, b: (b, p)),
            pl.BlockSpec((heads_per_step, 3, gq, gk), lambda p, b: (p, 0, 0, 0)),
        ],
        out_specs=pl.BlockSpec((SEQ, LANES), lambda p, b: (b, p)),
        compiler_params=_params(2),
        name="neighbourhood_attention",
    )(q, k, v, kc, vc, bias)


def _mix0_out_kernel(na_ref, yg_ref, h_ref, m_ref, wa_ref, wb_ref, g_ref, b_ref, o_ref):
    _, _, gate = _mod_rows(m_ref, 1)
    y = (jnp.dot(na_ref[...], wa_ref[...], preferred_element_type=F32)
         + jnp.dot(yg_ref[...], wb_ref[...], preferred_element_type=F32))
    o_ref[...] = _post_norm(h_ref[...], y, gate, 1.0, g_ref[...], b_ref[...])


def _mix0_out(na, yg, h, m, row_of_tile, w_out, g, b):
    n = h.shape[0]
    tm = TOKEN_TILE
    return pl.pallas_call(
        _mix0_out_kernel,
        out_shape=jax.ShapeDtypeStruct((n, D_MODEL), F32),
        grid=(n // tm,),
        in_specs=[
            pl.BlockSpec((tm, NA_WIDTH), lambda i: (i, 0)),
            pl.BlockSpec((tm, LRU_WIDTH), lambda i: (i, 0)),
            pl.BlockSpec((tm, D_MODEL), lambda i: (i, 0)),
            pl.BlockSpec((1, N_MOD, D_MODEL), lambda i: (row_of_tile(i), 0, 0)),
            _resident((NA_WIDTH, D_MODEL)),
            _resident((LRU_WIDTH, D_MODEL)),
            _resident((1, D_MODEL)),
            _resident((1, D_MODEL)),
        ],
        out_specs=pl.BlockSpec((tm, D_MODEL), lambda i: (i, 0)),
        compiler_params=_params(1),
        name="mix0_out_proj",
    )(na, yg, h, m, w_out[:NA_WIDTH], w_out[NA_WIDTH:], g, b)


def _mix1_kernel(tiles_per_seq, hp_ref, h_ref, hn_ref, m_ref, wi_ref, cw_ref, wo_ref, g_ref, b_ref,
                 o_ref):
    tm = h_ref.shape[0]
    n = tm + 2 * HALO
    i = pl.program_id(0)
    first = (i % tiles_per_seq) == 0
    last = (i % tiles_per_seq) == tiles_per_seq - 1
    shift, scale, gate = _mod_rows(m_ref, 1)
    x = h_ref[...]
    xf = jnp.concatenate([hp_ref[...], x, hn_ref[...]], axis=0)
    z = (xf * (1.0 + scale) + shift).astype(BF16)
    p = jnp.dot(z, wi_ref[...], preferred_element_type=F32)
    u = p[:, SC_WIDTH:2 * SC_WIDTH] * p[:, 2 * SC_WIDTH:]
    t = _row_index(n)
    outside = ((t < HALO) & first) | ((t >= tm + HALO) & last)
    u = jnp.where(outside, 0.0, u)
    cv = (cw_ref[0:1, :] * pltpu.roll(u, 1, 0) + cw_ref[1:2, :] * u
          + cw_ref[2:3, :] * pltpu.roll(u, n - 1, 0))
    yv = (p[HALO:HALO + tm, :SC_WIDTH] * cv[HALO:HALO + tm, :]).astype(BF16)
    y = jnp.dot(yv, wo_ref[...], preferred_element_type=F32)
    o_ref[...] = _post_norm(x, y, gate, 1.0, g_ref[...], b_ref[...])


def _mix1(h, m, row_of_tile, w_in, conv_w, w_out, g, b):
    n = h.shape[0]
    tm = TOKEN_TILE
    halo_per_tile = tm // HALO
    n_halo_blocks = n // HALO
    return pl.pallas_call(
        functools.partial(_mix1_kernel, SEQ // tm),
        out_shape=jax.ShapeDtypeStruct((n, D_MODEL), F32),
        grid=(n // tm,),
        in_specs=[
            pl.BlockSpec((HALO, D_MODEL), lambda i: (jnp.maximum(i * halo_per_tile - 1, 0), 0)),
            pl.BlockSpec((tm, D_MODEL), lambda i: (i, 0)),
            pl.BlockSpec((HALO, D_MODEL),
                         lambda i: (jnp.minimum((i + 1) * halo_per_tile, n_halo_blocks - 1), 0)),
            pl.BlockSpec((1, N_MOD, D_MODEL), lambda i: (row_of_tile(i), 0, 0)),
            _resident((D_MODEL, 3 * SC_WIDTH)),
            _resident((SC_K, SC_WIDTH)),
            _resident((SC_WIDTH, D_MODEL)),
            _resident((1, D_MODEL)),
            _resident((1, D_MODEL)),
        ],
        out_specs=pl.BlockSpec((tm, D_MODEL), lambda i: (i, 0)),
        compiler_params=_params(1),
        name="mix1_short_conv",
    )(h, h, h, m, w_in, conv_w, w_out, g, b)


def kernel(x, c, ctx, c_ctx, mod_w, mod_b, ln_g, ln_b, ffn_w1, ffn_w3, ffn_w2, mix0_w_in, na_rpb,
           lru_conv_w, lru_conv_b, lru_w_a, lru_b_a, lru_w_x, lru_b_x, lru_lambda, mix0_w_out,
           mix1_w_in, sconv_w, mix1_w_out):
    assert x.shape == (BATCH, SEQ, D_MODEL) and ctx.shape == (BATCH, CTX_LEN, D_MODEL)
    assert SEQ % TOKEN_TILE == 0 and (BATCH * CTX_LEN) % TOKEN_TILE == 0
    h = x.reshape(BATCH * SEQ, D_MODEL)
    hc = ctx.reshape(BATCH * CTX_LEN, D_MODEL)
    m_all = _modulation(c, c_ctx, mod_w, mod_b)

    tiles_per_seq = SEQ // TOKEN_TILE

    def sample_row(i):
        return i // tiles_per_seq

    def ctx_row(i):
        return CTX_ROW

    def ffn_weights(layer, idx):
        return (ffn_w1[layer, idx].astype(BF16), ffn_w3[layer, idx].astype(BF16),
                ffn_w2[layer, idx].astype(BF16))

    def ln(layer, sub):
        return ln_g[layer, sub][None, :], ln_b[layer, sub][None, :]

    m = m_all[0]
    w1, w3, w2 = ffn_weights(0, 0)
    g, b = ln(0, 0)
    h = _ffn(h, m, sample_row, w1, w3, w2, g, b, 0)
    hc = _ffn(hc, m, ctx_row, w1, w3, w2, g, b, 0)

    w_in = mix0_w_in[0].astype(BF16)
    q_scale = NA_HEAD_DIM ** -0.5
    q, k, v, xr, gr = _proj(h, m, sample_row, w_in, [
        (0, NA_WIDTH, BF16, q_scale),
        (NA_WIDTH, NA_WIDTH, BF16, 1.0),
        (2 * NA_WIDTH, NA_WIDTH, BF16, 1.0),
        (3 * NA_WIDTH, LRU_WIDTH, F32, 1.0),
        (3 * NA_WIDTH + LRU_WIDTH, LRU_WIDTH, F32, 1.0),
    ], 1)
    kc, vc, xrc = _proj(hc, m, ctx_row, w_in[:, NA_WIDTH:3 * NA_WIDTH + LRU_WIDTH], [
        (0, NA_WIDTH, BF16, 1.0),
        (NA_WIDTH, NA_WIDTH, BF16, 1.0),
        (2 * NA_WIDTH, LRU_WIDTH, F32, 1.0),
    ], 1)
    yg = _lru(xrc, xr, gr, lru_conv_w[0], lru_conv_b[0], lru_w_a[0], lru_b_a[0], lru_w_x[0],
              lru_b_x[0], lru_lambda[0])
    na = _neighbourhood_attention(q, k, v, kc, vc, na_rpb[0])
    g, b = ln(0, 1)
    h = _mix0_out(na, yg, h, m, sample_row, mix0_w_out[0].astype(BF16), g, b)
    w1, w3, w2 = ffn_weights(0, 1)
    g, b = ln(0, 2)
    h = _ffn(h, m, sample_row, w1, w3, w2, g, b, 2)

    m = m_all[1]
    w1, w3, w2 = ffn_weights(1, 0)
    g, b = ln(1, 0)
    h = _ffn(h, m, sample_row, w1, w3, w2, g, b, 0)
    g, b = ln(1, 1)
    h = _mix1(h, m, sample_row, mix1_w_in[0].astype(BF16), sconv_w[0],
              mix1_w_out[0].astype(BF16), g, b)
    w1, w3, w2 = ffn_weights(1, 1)
    g, b = ln(1, 2)
    h = _ffn(h, m, sample_row, w1, w3, w2, g, b, 2)
    return h.reshape(BATCH, SEQ, D_MODEL)
```

```python
import functools

import numpy as np
import jax
import jax.numpy as jnp
from jax import lax
from jax.experimental import pallas as pl
from jax.experimental.pallas import tpu as pltpu

F32 = jnp.float32
BF16 = jnp.bfloat16

D_MODEL = 1024
BATCH = 16
SEQ = 2048
DEPTH = 2
GRID_W = 64
CTX_LEN = 256

NA_HEADS = 8
NA_HEAD_DIM = 64
NA_WIDTH = NA_HEADS * NA_HEAD_DIM
NA_KH = 8
NA_KW = 16
LRU_WIDTH = 512
LRU_BLOCKS = 8
LRU_BLOCK = LRU_WIDTH // LRU_BLOCKS
LRU_CONV = 4
LRU_C = 8.0
SC_WIDTH = D_MODEL
SC_K = 3
D_FF = 2816
FFN_RES = 0.5
N_SUB = 3
N_MOD = 3 * N_SUB
ALPHA = (2.0 * DEPTH) ** 0.25
LN_EPS = 1e-5
NEG_INF = -1e30

ROWS = SEQ // GRID_W

SUBLANES = 8
LANES = 128
VMEM_LIMIT_BYTES = 56 * 1024 * 1024

MOD_ROWS = 24
CTX_ROW = BATCH
TOKEN_TILE = 512
NA_GROUP_ROWS = 4
NA_KEY_ROWS = 12
NA_GROUPS = ROWS // NA_GROUP_ROWS
LRU_CHUNK = 256
HALO = SUBLANES


def _params(n_axes):
    return pltpu.CompilerParams(dimension_semantics=("arbitrary",) * n_axes,
                                vmem_limit_bytes=VMEM_LIMIT_BYTES)


def _resident(shape):
    return pl.BlockSpec(shape, lambda *_: (0,) * len(shape), pipeline_mode=pl.Buffered(1))


def _sigmoid(x):
    return 1.0 / (1.0 + jnp.exp(-x))


def _mod_rows(m_ref, sub):
    shift = m_ref[0, 3 * sub:3 * sub + 1, :]
    scale = m_ref[0, 3 * sub + 1:3 * sub + 2, :]
    gate = m_ref[0, 3 * sub + 2:3 * sub + 3, :]
    return shift, scale, gate


def _post_norm(x, y, gate, res_w, g, b):
    r = ALPHA * x + (res_w * gate) * y
    mu = jnp.mean(r, axis=-1, keepdims=True)
    rc = r - mu
    var = jnp.mean(rc * rc, axis=-1, keepdims=True)
    return rc * lax.rsqrt(var + LN_EPS) * g + b


def _mod_kernel(c_ref, w_ref, b_ref, o_ref):
    cv = c_ref[...]
    cond = (cv * _sigmoid(cv)).astype(BF16)
    o_ref[...] = jnp.dot(cond, w_ref[...].astype(BF16), preferred_element_type=F32) + b_ref[...]


def _modulation(c, c_ctx, mod_w, mod_b):
    pad = jnp.zeros((MOD_ROWS - BATCH - 1, D_MODEL), F32)
    cond = jnp.concatenate([c, c_ctx[None, :], pad], axis=0)
    tn = D_MODEL
    out = pl.pallas_call(
        _mod_kernel,
        out_shape=jax.ShapeDtypeStruct((DEPTH, MOD_ROWS, N_MOD * D_MODEL), F32),
        grid=(DEPTH, N_MOD * D_MODEL // tn),
        in_specs=[
            pl.BlockSpec((MOD_ROWS, D_MODEL), lambda l, j: (0, 0)),
            pl.BlockSpec((None, D_MODEL, tn), lambda l, j: (l, 0, j)),
            pl.BlockSpec((None, 1, tn), lambda l, j: (l, 0, j)),
        ],
        out_specs=pl.BlockSpec((None, MOD_ROWS, tn), lambda l, j: (l, 0, j)),
        compiler_params=_params(2),
        name="modulation",
    )(cond, mod_w, mod_b.reshape(DEPTH, 1, N_MOD * D_MODEL))
    return out.reshape(DEPTH, MOD_ROWS, N_MOD, D_MODEL)


def _ffn_kernel(sub, h_ref, m_ref, w1_ref, w3_ref, w2_ref, g_ref, b_ref, o_ref):
    x = h_ref[...]
    shift, scale, gate = _mod_rows(m_ref, sub)
    z = (x * (1.0 + scale) + shift).astype(BF16)
    a = jnp.dot(z, w1_ref[...], preferred_element_type=F32)
    bb = jnp.dot(z, w3_ref[...], preferred_element_type=F32)
    u = (a * _sigmoid(a) * bb).astype(BF16)
    y = jnp.dot(u, w2_ref[...], preferred_element_type=F32)
    o_ref[...] = _post_norm(x, y, gate, FFN_RES, g_ref[...], b_ref[...])


def _ffn(h, m, row_of_tile, w1, w3, w2, g, b, sub):
    n = h.shape[0]
    tm = TOKEN_TILE
    return pl.pallas_call(
        functools.partial(_ffn_kernel, sub),
        out_shape=jax.ShapeDtypeStruct((n, D_MODEL), F32),
        grid=(n // tm,),
        in_specs=[
            pl.BlockSpec((tm, D_MODEL), lambda i: (i, 0)),
            pl.BlockSpec((1, N_MOD, D_MODEL), lambda i: (row_of_tile(i), 0, 0)),
            _resident((D_MODEL, D_FF)),
            _resident((D_MODEL, D_FF)),
            _resident((D_FF, D_MODEL)),
            _resident((1, D_MODEL)),
            _resident((1, D_MODEL)),
        ],
        out_specs=pl.BlockSpec((tm, D_MODEL), lambda i: (i, 0)),
        compiler_params=_params(1),
        name="ffn",
    )(h, m, w1, w3, w2, g, b)


def _proj_kernel(sub, outs, h_ref, m_ref, w_ref, *o_refs):
    shift, scale, _ = _mod_rows(m_ref, sub)
    z = (h_ref[...] * (1.0 + scale) + shift).astype(BF16)
    p = jnp.dot(z, w_ref[...], preferred_element_type=F32)
    for (start, width, _, mult), o_ref in zip(outs, o_refs):
        v = p[:, start:start + width]
        if mult != 1.0:
            v = v * mult
        o_ref[...] = v.astype(o_ref.dtype)


def _proj(h, m, row_of_tile, w, outs, sub):
    n = h.shape[0]
    tm = TOKEN_TILE
    return pl.pallas_call(
        functools.partial(_proj_kernel, sub, outs),
        out_shape=[jax.ShapeDtypeStruct((n, width), dt) for _, width, dt, _ in outs],
        grid=(n // tm,),
        in_specs=[
            pl.BlockSpec((tm, D_MODEL), lambda i: (i, 0)),
            pl.BlockSpec((1, N_MOD, D_MODEL), lambda i: (row_of_tile(i), 0, 0)),
            _resident(w.shape),
        ],
        out_specs=[pl.BlockSpec((tm, width), lambda i: (i, 0)) for _, width, _, _ in outs],
        compiler_params=_params(1),
        name="mix0_in_proj",
    )(h, m, w)


def _row_index(n):
    return lax.broadcasted_iota(jnp.int32, (n, 1), 0)


def _lru_coeffs(x_ref, n, cw_ref, cb_ref, wg_ref, bg_ref, log_sig, xc_s, a_s, b_s):
    x = x_ref[...]
    t = _row_index(n)
    xc = cw_ref[2:3, :] * x + cb_ref[...]
    xc = xc + cw_ref[0:1, :] * jnp.where(t >= 2, pltpu.roll(x, 2, 0), 0.0)
    xc = xc + cw_ref[1:2, :] * jnp.where(t >= 1, pltpu.roll(x, 1, 0), 0.0)
    xc = xc + cw_ref[3:4, :] * jnp.where(t < n - 1, pltpu.roll(x, n - 1, 0), 0.0)
    xc_s[0:n, :] = xc

    tin = _row_index(LRU_CHUNK) % SUBLANES
    for c0 in range(0, n, LRU_CHUNK):
        xcc = xc_s[c0:c0 + LRU_CHUNK, :]
        gts = jnp.dot(xcc.astype(BF16), wg_ref[...], preferred_element_type=F32) + bg_ref[...]
        for d in range(2):
            base = 2 * LRU_WIDTH * d
            gate_r = _sigmoid(gts[:, base:base + LRU_WIDTH])
            gate_i = _sigmoid(gts[:, base + LRU_WIDTH:base + 2 * LRU_WIDTH])
            log_a = (LRU_C * gate_r) * log_sig[d:d + 1, :]
            a = jnp.exp(log_a)
            bb = jnp.sqrt(-jnp.tanh(log_a) * (a * a + 1.0)) * (gate_i * xcc)
            for s in (1, 2, 4):
                if d == 0:
                    ok = tin >= s
                    sh = s
                else:
                    ok = tin < SUBLANES - s
                    sh = LRU_CHUNK - s
                a_n = jnp.where(ok, pltpu.roll(a, sh, 0), 1.0)
                b_n = jnp.where(ok, pltpu.roll(bb, sh, 0), 0.0)
                bb = a * b_n + bb
                a = a * a_n
            a_s[d, c0:c0 + LRU_CHUNK, :] = a
            b_s[d, c0:c0 + LRU_CHUNK, :] = bb


def _lru_carry(n, a_s, b_s, h_fwd, h_bwd):
    nblk = n // SUBLANES

    def body(j, carry):
        hf, hb = carry
        rf = pl.multiple_of(j * SUBLANES, SUBLANES)
        of = a_s[0, pl.ds(rf, SUBLANES), :] * hf + b_s[0, pl.ds(rf, SUBLANES), :]
        b_s[0, pl.ds(rf, SUBLANES), :] = of
        rb = pl.multiple_of((nblk - 1 - j) * SUBLANES, SUBLANES)
        ob = a_s[1, pl.ds(rb, SUBLANES), :] * hb + b_s[1, pl.ds(rb, SUBLANES), :]
        b_s[1, pl.ds(rb, SUBLANES), :] = ob
        return of[SUBLANES - 1:SUBLANES, :], ob[0:1, :]

    return lax.fori_loop(0, nblk, body, (h_fwd, h_bwd), unroll=4)


def _gelu_tanh(x):
    return 0.5 * x * (1.0 + jnp.tanh(0.7978845608028654 * (x + 0.044715 * (x * x * x))))


def _lru_kernel(xc_ref, xl_ref, gl_ref, cw_ref, cb_ref, wg_ref, bg_ref, lam_ref, o_ref,
                xc_s, a_s, b_s):
    lam = lam_ref[...]
    log_sig = jnp.minimum(lam, 0.0) - jnp.log1p(jnp.exp(-jnp.abs(lam)))
    zero = jnp.zeros((1, LRU_WIDTH), F32)
    _lru_coeffs(xc_ref, CTX_LEN, cw_ref, cb_ref, wg_ref, bg_ref, log_sig, xc_s, a_s, b_s)
    hf, hb = _lru_carry(CTX_LEN, a_s, b_s, zero, zero)
    _lru_coeffs(xl_ref, SEQ, cw_ref, cb_ref, wg_ref, bg_ref, log_sig, xc_s, a_s, b_s)
    _lru_carry(SEQ, a_s, b_s, hf, hb)
    for c0 in range(0, SEQ, LRU_CHUNK):
        y = b_s[0, c0:c0 + LRU_CHUNK, :] + b_s[1, c0:c0 + LRU_CHUNK, :]
        o_ref[c0:c0 + LRU_CHUNK, :] = (y * _gelu_tanh(gl_ref[c0:c0 + LRU_CHUNK, :])).astype(BF16)


def _block_diag(w):
    eye = jnp.eye(LRU_BLOCKS, dtype=w.dtype)
    return jnp.einsum('nkj,nm->nkmj', w, eye).reshape(LRU_WIDTH, LRU_WIDTH)


def _lru(xr_ctx, xr_lat, g_lat, conv_w, conv_b, w_a, b_a, w_x, b_x, lam):
    wg = jnp.concatenate([_block_diag(w_a[0]), _block_diag(w_x[0]),
                          _block_diag(w_a[1]), _block_diag(w_x[1])], axis=1).astype(BF16)
    bg = jnp.concatenate([b_a[0], b_x[0], b_a[1], b_x[1]])[None, :]
    return pl.pallas_call(
        _lru_kernel,
        out_shape=jax.ShapeDtypeStruct((BATCH * SEQ, LRU_WIDTH), BF16),
        grid=(BATCH,),
        in_specs=[
            pl.BlockSpec((CTX_LEN, LRU_WIDTH), lambda i: (i, 0)),
            pl.BlockSpec((SEQ, LRU_WIDTH), lambda i: (i, 0)),
            pl.BlockSpec((SEQ, LRU_WIDTH), lambda i: (i, 0)),
            _resident((LRU_CONV, LRU_WIDTH)),
            _resident((1, LRU_WIDTH)),
            _resident((LRU_WIDTH, 4 * LRU_WIDTH)),
            _resident((1, 4 * LRU_WIDTH)),
            _resident((2, LRU_WIDTH)),
        ],
        out_specs=pl.BlockSpec((SEQ, LRU_WIDTH), lambda i: (i, 0)),
        scratch_shapes=[
            pltpu.VMEM((SEQ, LRU_WIDTH), F32),
            pltpu.VMEM((2, SEQ, LRU_WIDTH), F32),
            pltpu.VMEM((2, SEQ, LRU_WIDTH), F32),
        ],
        compiler_params=_params(1),
        name="rglru",
    )(xr_ctx, xr_lat, g_lat, conv_w, conv_b[None, :], wg, bg, lam)


def _na_key_row_start(g):
    return int(np.clip(NA_GROUP_ROWS * g - NA_KH // 2, 0, ROWS - NA_KEY_ROWS))


def _na_variant(g):
    return 0 if g == 0 else (2 if g == NA_GROUPS - 1 else 1)


def _na_block_tables():
    masked_slot = 2 * NA_KH - 1
    tabs = {}
    for g in range(NA_GROUPS):
        r = (NA_GROUP_ROWS * g + np.arange(NA_GROUP_ROWS))[:, None]
        kr = (_na_key_row_start(g) + np.arange(NA_KEY_ROWS))[None, :]
        r0 = np.clip(r - NA_KH // 2, 0, ROWS - NA_KH)
        in_window = (kr >= r0) & (kr < r0 + NA_KH)
        tab = np.where(in_window, kr - r + NA_KH - 1, masked_slot)
        v = _na_variant(g)
        assert v not in tabs or np.array_equal(tabs[v], tab)
        tabs[v] = tab
    return np.stack([tabs[v] for v in sorted(tabs)])


def _na_bias(rpb):
    n_rel = 2 * NA_KH - 1
    period = 2 * GRID_W - 1
    lead = GRID_W - NA_KW
    padded = jnp.pad(rpb.astype(F32), ((0, 0), (0, 0), (lead, period - lead - (2 * NA_KW - 1))))
    tiled = jnp.tile(padded, (1, 1, GRID_W + 1))[..., :GRID_W * (period + 1)]
    toeplitz = tiled.reshape(NA_HEADS, n_rel, GRID_W, period + 1)[..., ::-1, :GRID_W]
    cq = np.arange(GRID_W)[:, None]
    kc = np.arange(GRID_W)[None, :]
    start = np.clip(cq - NA_KW // 2, 0, GRID_W - NA_KW)
    col_ok = (kc >= start) & (kc < start + NA_KW)
    blocks = jnp.where(col_ok, toeplitz, NEG_INF)
    blocks = jnp.concatenate([blocks, jnp.full((NA_HEADS, 1, GRID_W, GRID_W), NEG_INF, F32)], axis=1)
    variants = []
    for tab in _na_block_tables():
        rows = [jnp.concatenate([blocks[:, int(s)] for s in tab_row], axis=-1) for tab_row in tab]
        variants.append(jnp.concatenate(rows, axis=-2))
    return jnp.stack(variants, axis=1)


def _na_kernel(q_ref, k_ref, v_ref, kc_ref, vc_ref, bias_ref, o_ref):
    gq = NA_GROUP_ROWS * GRID_W
    gk = NA_KEY_ROWS * GRID_W
    lane = lax.broadcasted_iota(jnp.int32, (1, LANES), 1)
    kc = kc_ref[...]
    vc = vc_ref[...]
    contract_last = (((1,), (1,)), ((), ()))
    for g in range(NA_GROUPS):
        k0 = _na_key_row_start(g) * GRID_W
        q = q_ref[g * gq:(g + 1) * gq, :]
        kw = k_ref[k0:k0 + gk, :]
        vw = v_ref[k0:k0 + gk, :]
        outs = []
        for hh in range(LANES // NA_HEAD_DIM):
            head_lanes = (lane >= hh * NA_HEAD_DIM) & (lane < (hh + 1) * NA_HEAD_DIM)
            qh = jnp.where(head_lanes, q, jnp.zeros_like(q))
            s_loc = lax.dot_general(qh, kw, contract_last, preferred_element_type=F32)
            s_loc = s_loc + bias_ref[hh, _na_variant(g)]
            s_ctx = lax.dot_general(qh, kc, contract_last, preferred_element_type=F32)
            mx = jnp.maximum(jnp.max(s_loc, axis=-1, keepdims=True),
                             jnp.max(s_ctx, axis=-1, keepdims=True))
            p_loc = jnp.exp(s_loc - mx)
            p_ctx = jnp.exp(s_ctx - mx)
            den = jnp.sum(p_loc, axis=-1, keepdims=True) + jnp.sum(p_ctx, axis=-1, keepdims=True)
            o = (jnp.dot(p_loc.astype(BF16), vw, preferred_element_type=F32)
                 + jnp.dot(p_ctx.astype(BF16), vc, preferred_element_type=F32))
            outs.append((head_lanes, o / den))
        res = outs[0][1]
        for head_lanes, o in outs[1:]:
            res = jnp.where(head_lanes, o, res)
        o_ref[g * gq:(g + 1) * gq, :] = res.astype(BF16)


def _neighbourhood_attention(q, k, v, kc, vc, rpb):
    bias = _na_bias(rpb)
    heads_per_step = LANES // NA_HEAD_DIM
    n_pairs = NA_HEADS // heads_per_step
    gq = NA_GROUP_ROWS * GRID_W
    gk = NA_KEY_ROWS * GRID_W
    return pl.pallas_call(
        _na_kernel,
        out_shape=jax.ShapeDtypeStruct((BATCH * SEQ, NA_WIDTH), BF16),
        grid=(n_pairs, BATCH),
        in_specs=[
            pl.BlockSpec((SEQ, LANES), lambda p, b: (b, p)),
            pl.BlockSpec((SEQ, LANES), lambda p, b: (b, p)),
            pl.BlockSpec((SEQ, LANES), lambda p, b: (b, p)),
            pl.BlockSpec((CTX_LEN, LANES), lambda p, b: (b, p)),
            pl.BlockSpec((CTX_LEN, LANES), lambda p, b: (b, p)),
            pl.BlockSpec((heads_per_step, 3, gq, gk), lambda p, b: (p, 0, 0, 0)),
        ],
        out_specs=pl.BlockSpec((SEQ, LANES), lambda p, b: (b, p)),
        compiler_params=_params(2),
        name="neighbourhood_attention",
    )(q, k, v, kc, vc, bias)


def _mix0_out_kernel(na_ref, yg_ref, h_ref, m_ref, wa_ref, wb_ref, g_ref, b_ref, o_ref):
    _, _, gate = _mod_rows(m_ref, 1)
    y = (jnp.dot(na_ref[...], wa_ref[...], preferred_element_type=F32)
         + jnp.dot(yg_ref[...], wb_ref[...], preferred_element_type=F32))
    o_ref[...] = _post_norm(h_ref[...], y, gate, 1.0, g_ref[...], b_ref[...])


def _mix0_out(na, yg, h, m, row_of_tile, w_out, g, b):
    n = h.shape[0]
    tm = TOKEN_TILE
    return pl.pallas_call(
        _mix0_out_kernel,
        out_shape=jax.ShapeDtypeStruct((n, D_MODEL), F32),
        grid=(n // tm,),
        in_specs=[
            pl.BlockSpec((tm, NA_WIDTH), lambda i: (i, 0)),
            pl.BlockSpec((tm, LRU_WIDTH), lambda i: (i, 0)),
            pl.BlockSpec((tm, D_MODEL), lambda i: (i, 0)),
            pl.BlockSpec((1, N_MOD, D_MODEL), lambda i: (row_of_tile(i), 0, 0)),
            _resident((NA_WIDTH, D_MODEL)),
            _resident((LRU_WIDTH, D_MODEL)),
            _resident((1, D_MODEL)),
            _resident((1, D_MODEL)),
        ],
        out_specs=pl.BlockSpec((tm, D_MODEL), lambda i: (i, 0)),
        compiler_params=_params(1),
        name="mix0_out_proj",
    )(na, yg, h, m, w_out[:NA_WIDTH], w_out[NA_WIDTH:], g, b)


def _mix1_kernel(tiles_per_seq, hp_ref, h_ref, hn_ref, m_ref, wi_ref, cw_ref, wo_ref, g_ref, b_ref,
                 o_ref):
    tm = h_ref.shape[0]
    n = tm + 2 * HALO
    i = pl.program_id(0)
    first = (i % tiles_per_seq) == 0
    last = (i % tiles_per_seq) == tiles_per_seq - 1
    shift, scale, gate = _mod_rows(m_ref, 1)
    x = h_ref[...]
    xf = jnp.concatenate([hp_ref[...], x, hn_ref[...]], axis=0)
    z = (xf * (1.0 + scale) + shift).astype(BF16)
    p = jnp.dot(z, wi_ref[...], preferred_element_type=F32)
    u = p[:, SC_WIDTH:2 * SC_WIDTH] * p[:, 2 * SC_WIDTH:]
    t = _row_index(n)
    outside = ((t < HALO) & first) | ((t >= tm + HALO) & last)
    u = jnp.where(outside, 0.0, u)
    cv = (cw_ref[0:1, :] * pltpu.roll(u, 1, 0) + cw_ref[1:2, :] * u
          + cw_ref[2:3, :] * pltpu.roll(u, n - 1, 0))
    yv = (p[HALO:HALO + tm, :SC_WIDTH] * cv[HALO:HALO + tm, :]).astype(BF16)
    y = jnp.dot(yv, wo_ref[...], preferred_element_type=F32)
    o_ref[...] = _post_norm(x, y, gate, 1.0, g_ref[...], b_ref[...])


def _mix1(h, m, row_of_tile, w_in, conv_w, w_out, g, b):
    n = h.shape[0]
    tm = TOKEN_TILE
    halo_per_tile = tm // HALO
    n_halo_blocks = n // HALO
    return pl.pallas_call(
        functools.partial(_mix1_kernel, SEQ // tm),
        out_shape=jax.ShapeDtypeStruct((n, D_MODEL), F32),
        grid=(n // tm,),
        in_specs=[
            pl.BlockSpec((HALO, D_MODEL), lambda i: (jnp.maximum(i * halo_per_tile - 1, 0), 0)),
            pl.BlockSpec((tm, D_MODEL), lambda i: (i, 0)),
            pl.BlockSpec((HALO, D_MODEL),
                         lambda i: (jnp.minimum((i + 1) * halo_per_tile, n_halo_blocks - 1), 0)),
            pl.BlockSpec((1, N_MOD, D_MODEL), lambda i: (row_of_tile(i), 0, 0)),
            _resident((D_MODEL, 3 * SC_WIDTH)),
            _resident((SC_K, SC_WIDTH)),
            _resident((SC_WIDTH, D_MODEL)),
            _resident((1, D_MODEL)),
            _resident((1, D_MODEL)),
        ],
        out_specs=pl.BlockSpec((tm, D_MODEL), lambda i: (i, 0)),
        compiler_params=_params(1),
        name="mix1_short_conv",
    )(h, h, h, m, w_in, conv_w, w_out, g, b)


def kernel(x, c, ctx, c_ctx, mod_w, mod_b, ln_g, ln_b, ffn_w1, ffn_w3, ffn_w2, mix0_w_in, na_rpb,
           lru_conv_w, lru_conv_b, lru_w_a, lru_b_a, lru_w_x, lru_b_x, lru_lambda, mix0_w_out,
           mix1_w_in, sconv_w, mix1_w_out):
    assert x.shape == (BATCH, SEQ, D_MODEL) and ctx.shape == (BATCH, CTX_LEN, D_MODEL)
    assert SEQ % TOKEN_TILE == 0 and (BATCH * CTX_LEN) % TOKEN_TILE == 0
    h = x.reshape(BATCH * SEQ, D_MODEL)
    hc = ctx.reshape(BATCH * CTX_LEN, D_MODEL)
    m_all = _modulation(c, c_ctx, mod_w, mod_b)

    tiles_per_seq = SEQ // TOKEN_TILE

    def sample_row(i):
        return i // tiles_per_seq

    def ctx_row(i):
        return CTX_ROW

    def ffn_weights(layer, idx):
        return (ffn_w1[layer, idx].astype(BF16), ffn_w3[layer, idx].astype(BF16),
                ffn_w2[layer, idx].astype(BF16))

    def ln(layer, sub):
        return ln_g[layer, sub][None, :], ln_b[layer, sub][None, :]

    m = m_all[0]
    w1, w3, w2 = ffn_weights(0, 0)
    g, b = ln(0, 0)
    h = _ffn(h, m, sample_row, w1, w3, w2, g, b, 0)
    hc = _ffn(hc, m, ctx_row, w1, w3, w2, g, b, 0)

    w_in = mix0_w_in[0].astype(BF16)
    q_scale = NA_HEAD_DIM ** -0.5
    q, k, v, xr, gr = _proj(h, m, sample_row, w_in, [
        (0, NA_WIDTH, BF16, q_scale),
        (NA_WIDTH, NA_WIDTH, BF16, 1.0),
        (2 * NA_WIDTH, NA_WIDTH, BF16, 1.0),
        (3 * NA_WIDTH, LRU_WIDTH, F32, 1.0),
        (3 * NA_WIDTH + LRU_WIDTH, LRU_WIDTH, F32, 1.0),
    ], 1)
    kc, vc, xrc = _proj(hc, m, ctx_row, w_in[:, NA_WIDTH:3 * NA_WIDTH + LRU_WIDTH], [
        (0, NA_WIDTH, BF16, 1.0),
        (NA_WIDTH, NA_WIDTH, BF16, 1.0),
        (2 * NA_WIDTH, LRU_WIDTH, F32, 1.0),
    ], 1)
    yg = _lru(xrc, xr, gr, lru_conv_w[0], lru_conv_b[0], lru_w_a[0], lru_b_a[0], lru_w_x[0],
              lru_b_x[0], lru_lambda[0])
    na = _neighbourhood_attention(q, k, v, kc, vc, na_rpb[0])
    g, b = ln(0, 1)
    h = _mix0_out(na, yg, h, m, sample_row, mix0_w_out[0].astype(BF16), g, b)
    w1, w3, w2 = ffn_weights(0, 1)
    g, b = ln(0, 2)
    h = _ffn(h, m, sample_row, w1, w3, w2, g, b, 2)

    m = m_all[1]
    w1, w3, w2 = ffn_weights(1, 0)
    g, b = ln(1, 0)
    h = _ffn(h, m, sample_row, w1, w3, w2, g, b, 0)
    g, b = ln(1, 1)
    h = _mix1(h, m, sample_row, mix1_w_in[0].astype(BF16), sconv_w[0],
              mix1_w_out[0].astype(BF16), g, b)
    w1, w3, w2 = ffn_weights(1, 1)
    g, b = ln(1, 2)
    h = _ffn(h, m, sample_row, w1, w3, w2, g, b, 2)
    return h.reshape(BATCH, SEQ, D_MODEL)
```

```python
import functools

import numpy as np
import jax
import jax.numpy as jnp
from jax import lax
from jax.experimental import pallas as pl
from jax.experimental.pallas import tpu as pltpu

F32 = jnp.float32
BF16 = jnp.bfloat16

D_MODEL = 1024
BATCH = 16
SEQ = 2048
DEPTH = 2
GRID_W = 64
CTX_LEN = 256

NA_HEADS = 8
NA_HEAD_DIM = 64
NA_WIDTH = NA_HEADS * NA_HEAD_DIM
NA_KH = 8
NA_KW = 16
LRU_WIDTH = 512
LRU_BLOCKS = 8
LRU_BLOCK = LRU_WIDTH // LRU_BLOCKS
LRU_CONV = 4
LRU_CONV_LEFT = 2
LRU_C = 8.0
SC_WIDTH = D_MODEL
SC_K = 3
D_FF = 2816
FFN_RES = 0.5
N_SUB = 3
N_MOD = 3 * N_SUB
ALPHA = (2.0 * DEPTH) ** 0.25
LN_EPS = 1e-5
NEG_INF = -1e30

ROWS = SEQ // GRID_W

SUBLANES = 8
LANES = 128
VMEM_LIMIT_BYTES = 56 * 1024 * 1024

MOD_ROWS = 24
CTX_ROW = BATCH
TOKEN_TILE = 512
NA_GROUP_ROWS = 4
NA_KEY_ROWS = 12
NA_GROUPS = ROWS // NA_GROUP_ROWS
LRU_SEGMENTS = SUBLANES
LRU_CHUNK = 256
HALO = SUBLANES


def _params(n_axes):
    return pltpu.CompilerParams(dimension_semantics=("arbitrary",) * n_axes,
                                vmem_limit_bytes=VMEM_LIMIT_BYTES)


def _resident(shape):
    return pl.BlockSpec(shape, lambda *_: (0,) * len(shape), pipeline_mode=pl.Buffered(1))


def _sigmoid(x):
    return 1.0 / (1.0 + jnp.exp(-x))


def _mod_rows(m_ref, sub):
    shift = m_ref[0, 3 * sub:3 * sub + 1, :]
    scale = m_ref[0, 3 * sub + 1:3 * sub + 2, :]
    gate = m_ref[0, 3 * sub + 2:3 * sub + 3, :]
    return shift, scale, gate


def _post_norm(x, y, gate, res_w, g, b):
    r = ALPHA * x + (res_w * gate) * y
    mu = jnp.mean(r, axis=-1, keepdims=True)
    rc = r - mu
    var = jnp.mean(rc * rc, axis=-1, keepdims=True)
    return rc * lax.rsqrt(var + LN_EPS) * g + b


def _mod_kernel(c_ref, w_ref, b_ref, o_ref):
    cv = c_ref[...]
    cond = (cv * _sigmoid(cv)).astype(BF16)
    o_ref[...] = jnp.dot(cond, w_ref[...].astype(BF16), preferred_element_type=F32) + b_ref[...]


def _modulation(c, c_ctx, mod_w, mod_b):
    pad = jnp.zeros((MOD_ROWS - BATCH - 1, D_MODEL), F32)
    cond = jnp.concatenate([c, c_ctx[None, :], pad], axis=0)
    tn = D_MODEL
    out = pl.pallas_call(
        _mod_kernel,
        out_shape=jax.ShapeDtypeStruct((DEPTH, MOD_ROWS, N_MOD * D_MODEL), F32),
        grid=(DEPTH, N_MOD * D_MODEL // tn),
        in_specs=[
            pl.BlockSpec((MOD_ROWS, D_MODEL), lambda l, j: (0, 0)),
            pl.BlockSpec((None, D_MODEL, tn), lambda l, j: (l, 0, j)),
            pl.BlockSpec((None, 1, tn), lambda l, j: (l, 0, j)),
        ],
        out_specs=pl.BlockSpec((None, MOD_ROWS, tn), lambda l, j: (l, 0, j)),
        compiler_params=_params(2),
        name="modulation",
    )(cond, mod_w, mod_b.reshape(DEPTH, 1, N_MOD * D_MODEL))
    return out.reshape(DEPTH, MOD_ROWS, N_MOD, D_MODEL)


def _ffn_kernel(sub, h_ref, m_ref, w1_ref, w3_ref, w2_ref, g_ref, b_ref, o_ref):
    x = h_ref[...]
    shift, scale, gate = _mod_rows(m_ref, sub)
    z = (x * (1.0 + scale) + shift).astype(BF16)
    a = jnp.dot(z, w1_ref[...], preferred_element_type=F32)
    bb = jnp.dot(z, w3_ref[...], preferred_element_type=F32)
    u = (a * _sigmoid(a) * bb).astype(BF16)
    y = jnp.dot(u, w2_ref[...], preferred_element_type=F32)
    o_ref[...] = _post_norm(x, y, gate, FFN_RES, g_ref[...], b_ref[...])


def _ffn(h, m, row_of_tile, w1, w3, w2, g, b, sub):
    n = h.shape[0]
    tm = TOKEN_TILE
    return pl.pallas_call(
        functools.partial(_ffn_kernel, sub),
        out_shape=jax.ShapeDtypeStruct((n, D_MODEL), F32),
        grid=(n // tm,),
        in_specs=[
            pl.BlockSpec((tm, D_MODEL), lambda i: (i, 0)),
            pl.BlockSpec((1, N_MOD, D_MODEL), lambda i: (row_of_tile(i), 0, 0)),
            _resident((D_MODEL, D_FF)),
            _resident((D_MODEL, D_FF)),
            _resident((D_FF, D_MODEL)),
            _resident((1, D_MODEL)),
            _resident((1, D_MODEL)),
        ],
        out_specs=pl.BlockSpec((tm, D_MODEL), lambda i: (i, 0)),
        compiler_params=_params(1),
        name="ffn",
    )(h, m, w1, w3, w2, g, b)


def _segment_pitch(length):
    return length if (length // SUBLANES) % 2 == 1 else length + SUBLANES


def _proj_kernel(sub, outs, h_ref, m_ref, w_ref, *o_refs):
    shift, scale, _ = _mod_rows(m_ref, sub)
    z = (h_ref[...] * (1.0 + scale) + shift).astype(BF16)
    p = jnp.dot(z, w_ref[...], preferred_element_type=F32)
    tm = h_ref.shape[0]
    for (start, width, _, mult, seg_len), o_ref in zip(outs, o_refs):
        v = p[:, start:start + width]
        if mult != 1.0:
            v = v * mult
        v = v.astype(o_ref.dtype)
        if seg_len is None:
            o_ref[...] = v
            continue
        pitch = _segment_pitch(seg_len)
        for j in range(tm // seg_len):
            o_ref[j * pitch:j * pitch + seg_len, :] = v[j * seg_len:(j + 1) * seg_len, :]
            if pitch > seg_len:
                o_ref[j * pitch + seg_len:(j + 1) * pitch, :] = jnp.zeros(
                    (pitch - seg_len, width), o_ref.dtype)


def _proj(h, m, row_of_tile, w, outs, sub):
    n = h.shape[0]
    tm = TOKEN_TILE

    def out_rows(rows, seg_len):
        return rows if seg_len is None else rows // seg_len * _segment_pitch(seg_len)

    return pl.pallas_call(
        functools.partial(_proj_kernel, sub, outs),
        out_shape=[jax.ShapeDtypeStruct((out_rows(n, sl), width), dt) for _, width, dt, _, sl in outs],
        grid=(n // tm,),
        in_specs=[
            pl.BlockSpec((tm, D_MODEL), lambda i: (i, 0)),
            pl.BlockSpec((1, N_MOD, D_MODEL), lambda i: (row_of_tile(i), 0, 0)),
            _resident(w.shape),
        ],
        out_specs=[pl.BlockSpec((out_rows(tm, sl), width), lambda i: (i, 0))
                   for _, width, _, _, sl in outs],
        compiler_params=_params(1),
        name="mix0_in_proj",
    )(h, m, w)


def _row_index(n):
    return lax.broadcasted_iota(jnp.int32, (n, 1), 0)


def _load_segment_rows(slabs, v, pitch):
    return jnp.concatenate([slab[pl.ds(v, LRU_SEGMENTS, stride=pitch), :] for slab in slabs], axis=1)


def _tanh_sigmoid(x):
    return 0.5 * jnp.tanh(0.5 * x) + 0.5


def _lru_coeffs(slabs, length, cw_ref, cb_ref, wg_ref, bg_ref, log_sig, xs, a_s, b_s):
    pitch = _segment_pitch(length)
    rows = LRU_SEGMENTS * length
    lead = LRU_CONV_LEFT * LRU_SEGMENTS
    for v in range(length):
        xs[lead + v * LRU_SEGMENTS:lead + (v + 1) * LRU_SEGMENTS, :] = _load_segment_rows(slabs, v, pitch)
    seg = _row_index(LRU_SEGMENTS)
    for v in range(-LRU_CONV_LEFT, 0):
        src = xs[lead + (length + v) * LRU_SEGMENTS:lead + (length + v + 1) * LRU_SEGMENTS, :]
        xs[lead + v * LRU_SEGMENTS:lead + (v + 1) * LRU_SEGMENTS, :] = jnp.where(
            seg >= 1, pltpu.roll(src, 1, 0), 0.0)
    for v in range(LRU_CONV - 1 - LRU_CONV_LEFT):
        src = xs[lead + v * LRU_SEGMENTS:lead + (v + 1) * LRU_SEGMENTS, :]
        xs[lead + rows + v * LRU_SEGMENTS:lead + rows + (v + 1) * LRU_SEGMENTS, :] = jnp.where(
            seg < LRU_SEGMENTS - 1, pltpu.roll(src, LRU_SEGMENTS - 1, 0), 0.0)

    chunk = min(rows, LRU_CHUNK)
    for c0 in range(0, rows, chunk):
        xc = cb_ref[...]
        for k in range(LRU_CONV):
            off = c0 + k * LRU_SEGMENTS
            xc = xc + cw_ref[k:k + 1, :] * xs[off:off + chunk, :]
        gts = jnp.dot(xc.astype(BF16), wg_ref[...], preferred_element_type=F32) + bg_ref[...]
        for d in range(2):
            base = 2 * LRU_WIDTH * d
            gate_r = _tanh_sigmoid(gts[:, base:base + LRU_WIDTH])
            gate_i = _tanh_sigmoid(gts[:, base + LRU_WIDTH:base + 2 * LRU_WIDTH])
            log_a = (LRU_C * gate_r) * log_sig[d:d + 1, :]
            a = jnp.exp(log_a)
            a_s[d, c0:c0 + chunk, :] = a
            b_s[d, c0:c0 + chunk, :] = jnp.sqrt(-jnp.tanh(log_a) * (a * a + 1.0)) * (gate_i * xc)


def _lru_scan(length, a_s, b_s):
    zeros = jnp.zeros((LRU_SEGMENTS, LRU_WIDTH), F32)
    ones = jnp.ones((LRU_SEGMENTS, LRU_WIDTH), F32)

    def body(j, carry):
        hf, pf, hb, pb = carry
        rf = pl.ds(pl.multiple_of(j * LRU_SEGMENTS, LRU_SEGMENTS), LRU_SEGMENTS)
        af = a_s[0, rf, :]
        hf = af * hf + b_s[0, rf, :]
        pf = af * pf
        b_s[0, rf, :] = hf
        a_s[0, rf, :] = pf
        rb = pl.ds(pl.multiple_of((length - 1 - j) * LRU_SEGMENTS, LRU_SEGMENTS), LRU_SEGMENTS)
        ab = a_s[1, rb, :]
        hb = ab * hb + b_s[1, rb, :]
        pb = ab * pb
        b_s[1, rb, :] = hb
        a_s[1, rb, :] = pb
        return hf, pf, hb, pb

    return lax.fori_loop(0, length, body, (zeros, ones, zeros, ones), unroll=4)


def _segment_carries(h_end, p_end, h_in, order):
    seg = _row_index(LRU_SEGMENTS)
    carries = jnp.zeros((LRU_SEGMENTS, LRU_WIDTH), F32)
    for s in order:
        carries = jnp.where(seg == s, h_in, carries)
        h_in = h_end[s:s + 1, :] + p_end[s:s + 1, :] * h_in
    return carries, h_in


def _gelu_tanh(x):
    return 0.5 * x * (1.0 + jnp.tanh(0.7978845608028654 * (x + 0.044715 * (x * x * x))))


def _lru_kernel(*refs):
    n_slab = LRU_WIDTH // LANES
    xc_slabs, xl_slabs, gl_slabs = refs[0:n_slab], refs[n_slab:2 * n_slab], refs[2 * n_slab:3 * n_slab]
    cw_ref, cb_ref, wg_ref, bg_ref, lam_ref = refs[3 * n_slab:3 * n_slab + 5]
    o_slabs = refs[3 * n_slab + 5:4 * n_slab + 5]
    xs, a_s, b_s = refs[4 * n_slab + 5:]
    lam = lam_ref[...]
    log_sig = jnp.minimum(lam, 0.0) - jnp.log1p(jnp.exp(-jnp.abs(lam)))
    zero = jnp.zeros((1, LRU_WIDTH), F32)
    fwd_order = range(LRU_SEGMENTS)
    bwd_order = range(LRU_SEGMENTS - 1, -1, -1)

    ctx_len = CTX_LEN // LRU_SEGMENTS
    _lru_coeffs(xc_slabs, ctx_len, cw_ref, cb_ref, wg_ref, bg_ref, log_sig, xs, a_s, b_s)
    hf, pf, hb, pb = _lru_scan(ctx_len, a_s, b_s)
    _, hf0 = _segment_carries(hf, pf, zero, fwd_order)
    _, hb0 = _segment_carries(hb, pb, zero, bwd_order)

    length = SEQ // LRU_SEGMENTS
    pitch = _segment_pitch(length)
    _lru_coeffs(xl_slabs, length, cw_ref, cb_ref, wg_ref, bg_ref, log_sig, xs, a_s, b_s)
    hf, pf, hb, pb = _lru_scan(length, a_s, b_s)
    cf, _ = _segment_carries(hf, pf, hf0, fwd_order)
    cb, _ = _segment_carries(hb, pb, hb0, bwd_order)

    rows = LRU_SEGMENTS * length
    chunk = min(rows, LRU_CHUNK)
    nv = chunk // LRU_SEGMENTS

    def carried(prod, carry):
        return (prod.reshape(nv, LRU_SEGMENTS, LRU_WIDTH) * carry[None]).reshape(chunk, LRU_WIDTH)

    for c0 in range(0, rows, chunk):
        y = (b_s[0, c0:c0 + chunk, :] + carried(a_s[0, c0:c0 + chunk, :], cf)
             + b_s[1, c0:c0 + chunk, :] + carried(a_s[1, c0:c0 + chunk, :], cb))
        v0 = c0 // LRU_SEGMENTS
        gate = jnp.concatenate([_load_segment_rows(gl_slabs, v0 + i, pitch) for i in range(nv)], axis=0)
        out = y * _gelu_tanh(gate)
        for i in range(nv):
            for cg, o_ref in enumerate(o_slabs):
                o_ref[pl.ds(v0 + i, LRU_SEGMENTS, stride=pitch), :] = out[
                    i * LRU_SEGMENTS:(i + 1) * LRU_SEGMENTS, cg * LANES:(cg + 1) * LANES]
    if pitch > length:
        for o_ref in o_slabs:
            for s in range(LRU_SEGMENTS):
                o_ref[s * pitch + length:(s + 1) * pitch, :] = jnp.zeros((pitch - length, LANES), F32)


def _block_diag(w):
    eye = jnp.eye(LRU_BLOCKS, dtype=w.dtype)
    return jnp.einsum('nkj,nm->nkmj', w, eye).reshape(LRU_WIDTH, LRU_WIDTH)


def _lru(xr_ctx, xr_lat, g_lat, conv_w, conv_b, w_a, b_a, w_x, b_x, lam):
    wg = jnp.concatenate([_block_diag(w_a[0]), _block_diag(w_x[0]),
                          _block_diag(w_a[1]), _block_diag(w_x[1])], axis=1).astype(BF16)
    bg = jnp.concatenate([b_a[0], b_x[0], b_a[1], b_x[1]])[None, :]
    n_slab = LRU_WIDTH // LANES
    rows_c = LRU_SEGMENTS * _segment_pitch(CTX_LEN // LRU_SEGMENTS)
    rows_l = LRU_SEGMENTS * _segment_pitch(SEQ // LRU_SEGMENTS)

    def slab_specs(rows):
        return [pl.BlockSpec((rows, LANES), functools.partial(lambda cg, i: (i, cg), cg))
                for cg in range(n_slab)]

    scan_rows = SEQ
    return pl.pallas_call(
        _lru_kernel,
        out_shape=[jax.ShapeDtypeStruct((BATCH * rows_l, LANES), F32)] * n_slab,
        grid=(BATCH,),
        in_specs=slab_specs(rows_c) + slab_specs(rows_l) + slab_specs(rows_l) + [
            _resident((LRU_CONV, LRU_WIDTH)),
            _resident((1, LRU_WIDTH)),
            _resident((LRU_WIDTH, 4 * LRU_WIDTH)),
            _resident((1, 4 * LRU_WIDTH)),
            _resident((2, LRU_WIDTH)),
        ],
        out_specs=[pl.BlockSpec((rows_l, LANES), lambda i: (i, 0))] * n_slab,
        scratch_shapes=[
            pltpu.VMEM((scan_rows + (LRU_CONV - 1) * LRU_SEGMENTS, LRU_WIDTH), F32),
            pltpu.VMEM((2, scan_rows, LRU_WIDTH), F32),
            pltpu.VMEM((2, scan_rows, LRU_WIDTH), F32),
        ],
        compiler_params=_params(1),
        name="rglru",
    )(*([xr_ctx] * n_slab + [xr_lat] * n_slab + [g_lat] * n_slab),
      conv_w, conv_b[None, :], wg, bg, lam)


def _na_key_row_start(g):
    return int(np.clip(NA_GROUP_ROWS * g - NA_KH // 2, 0, ROWS - NA_KEY_ROWS))


def _na_variant(g):
    return 0 if g == 0 else (2 if g == NA_GROUPS - 1 else 1)


def _na_block_tables():
    masked_slot = 2 * NA_KH - 1
    tabs = {}
    for g in range(NA_GROUPS):
        r = (NA_GROUP_ROWS * g + np.arange(NA_GROUP_ROWS))[:, None]
        kr = (_na_key_row_start(g) + np.arange(NA_KEY_ROWS))[None, :]
        r0 = np.clip(r - NA_KH // 2, 0, ROWS - NA_KH)
        in_window = (kr >= r0) & (kr < r0 + NA_KH)
        tab = np.where(in_window, kr - r + NA_KH - 1, masked_slot)
        v = _na_variant(g)
        assert v not in tabs or np.array_equal(tabs[v], tab)
        tabs[v] = tab
    return np.stack([tabs[v] for v in sorted(tabs)])


def _na_bias(rpb):
    n_rel = 2 * NA_KH - 1
    period = 2 * GRID_W - 1
    lead = GRID_W - NA_KW
    padded = jnp.pad(rpb.astype(F32), ((0, 0), (0, 0), (lead, period - lead - (2 * NA_KW - 1))))
    tiled = jnp.tile(padded, (1, 1, GRID_W + 1))[..., :GRID_W * (period + 1)]
    toeplitz = tiled.reshape(NA_HEADS, n_rel, GRID_W, period + 1)[..., ::-1, :GRID_W]
    cq = np.arange(GRID_W)[:, None]
    kc = np.arange(GRID_W)[None, :]
    start = np.clip(cq - NA_KW // 2, 0, GRID_W - NA_KW)
    col_ok = (kc >= start) & (kc < start + NA_KW)
    blocks = jnp.where(col_ok, toeplitz, NEG_INF)
    blocks = jnp.concatenate([blocks, jnp.full((NA_HEADS, 1, GRID_W, GRID_W), NEG_INF, F32)], axis=1)
    variants = []
    for tab in _na_block_tables():
        rows = [jnp.concatenate([blocks[:, int(s)] for s in tab_row], axis=-1) for tab_row in tab]
        variants.append(jnp.concatenate(rows, axis=-2))
    return jnp.stack(variants, axis=1)


def _na_kernel(q_ref, k_ref, v_ref, kc_ref, vc_ref, bias_ref, o_ref):
    gq = NA_GROUP_ROWS * GRID_W
    gk = NA_KEY_ROWS * GRID_W
    lane = lax.broadcasted_iota(jnp.int32, (1, LANES), 1)
    kc = kc_ref[...]
    vc = vc_ref[...]
    contract_last = (((1,), (1,)), ((), ()))
    for g in range(NA_GROUPS):
        k0 = _na_key_row_start(g) * GRID_W
        q = q_ref[g * gq:(g + 1) * gq, :]
        kw = k_ref[k0:k0 + gk, :]
        vw = v_ref[k0:k0 + gk, :]
        outs = []
        for hh in range(LANES // NA_HEAD_DIM):
            head_lanes = (lane >= hh * NA_HEAD_DIM) & (lane < (hh + 1) * NA_HEAD_DIM)
            qh = jnp.where(head_lanes, q, jnp.zeros_like(q))
            s_loc = lax.dot_general(qh, kw, contract_last, preferred_element_type=F32)
            s_loc = s_loc + bias_ref[hh, _na_variant(g)]
            s_ctx = lax.dot_general(qh, kc, contract_last, preferred_element_type=F32)
            mx = jnp.maximum(jnp.max(s_loc, axis=-1, keepdims=True),
                             jnp.max(s_ctx, axis=-1, keepdims=True))
            p_loc = jnp.exp(s_loc - mx)
            p_ctx = jnp.exp(s_ctx - mx)
            den = jnp.sum(p_loc, axis=-1, keepdims=True) + jnp.sum(p_ctx, axis=-1, keepdims=True)
            o = (jnp.dot(p_loc.astype(BF16), vw, preferred_element_type=F32)
                 + jnp.dot(p_ctx.astype(BF16), vc, preferred_element_type=F32))
            outs.append((head_lanes, o / den))
        res = outs[0][1]
        for head_lanes, o in outs[1:]:
            res = jnp.where(head_lanes, o, res)
        o_ref[g * gq:(g + 1) * gq, :] = res.astype(BF16)


def _neighbourhood_attention(q, k, v, kc, vc, rpb):
    bias = _na_bias(rpb)
    heads_per_step = LANES // NA_HEAD_DIM
    n_pairs = NA_HEADS // heads_per_step
    gq = NA_GROUP_ROWS * GRID_W
    gk = NA_KEY_ROWS * GRID_W
    return pl.pallas_call(
        _na_kernel,
        out_shape=jax.ShapeDtypeStruct((BATCH * SEQ, NA_WIDTH), BF16),
        grid=(n_pairs, BATCH),
        in_specs=[
            pl.BlockSpec((SEQ, LANES), lambda p, b: (b, p)),
            pl.BlockSpec((SEQ, LANES), lambda p, b: (b, p)),
            pl.BlockSpec((SEQ, LANES), lambda p, b: (b, p)),
            pl.BlockSpec((CTX_LEN, LANES), lambda p, b: (b, p)),
            pl.BlockSpec((CTX_LEN, LANES), lambda p, b: (b, p)),
            pl.BlockSpec((heads_per_step, 3, gq, gk), lambda p, b: (p, 0, 0, 0)),
        ],
        out_specs=pl.BlockSpec((SEQ, LANES), lambda p, b: (b, p)),
        compiler_params=_params(2),
        name="neighbourhood_attention",
    )(q, k, v, kc, vc, bias)


def _mix0_out_kernel(seg_len, *refs):
    n_slab = LRU_WIDTH // LANES
    na_ref = refs[0]
    yg_slabs = refs[1:1 + n_slab]
    h_ref, m_ref, wa_ref, wb_ref, g_ref, b_ref, o_ref = refs[1 + n_slab:]
    tm = h_ref.shape[0]
    pitch = _segment_pitch(seg_len)
    _, _, gate = _mod_rows(m_ref, 1)
    yg = jnp.concatenate([
        jnp.concatenate([slab[j * pitch:j * pitch + seg_len, :] for slab in yg_slabs], axis=1)
        for j in range(tm // seg_len)], axis=0).astype(BF16)
    y = (jnp.dot(na_ref[...], wa_ref[...], preferred_element_type=F32)
         + jnp.dot(yg, wb_ref[...], preferred_element_type=F32))
    o_ref[...] = _post_norm(h_ref[...], y, gate, 1.0, g_ref[...], b_ref[...])


def _mix0_out(na, yg_slabs, h, m, row_of_tile, w_out, g, b):
    n = h.shape[0]
    tm = TOKEN_TILE
    seg_len = SEQ // LRU_SEGMENTS
    slab_rows = tm // seg_len * _segment_pitch(seg_len)
    return pl.pallas_call(
        functools.partial(_mix0_out_kernel, seg_len),
        out_shape=jax.ShapeDtypeStruct((n, D_MODEL), F32),
        grid=(n // tm,),
        in_specs=[pl.BlockSpec((tm, NA_WIDTH), lambda i: (i, 0))]
        + [pl.BlockSpec((slab_rows, LANES), lambda i: (i, 0))] * len(yg_slabs)
        + [
            pl.BlockSpec((tm, D_MODEL), lambda i: (i, 0)),
            pl.BlockSpec((1, N_MOD, D_MODEL), lambda i: (row_of_tile(i), 0, 0)),
            _resident((NA_WIDTH, D_MODEL)),
            _resident((LRU_WIDTH, D_MODEL)),
            _resident((1, D_MODEL)),
            _resident((1, D_MODEL)),
        ],
        out_specs=pl.BlockSpec((tm, D_MODEL), lambda i: (i, 0)),
        compiler_params=_params(1),
        name="mix0_out_proj",
    )(na, *yg_slabs, h, m, w_out[:NA_WIDTH], w_out[NA_WIDTH:], g, b)


def _mix1_kernel(tiles_per_seq, hp_ref, h_ref, hn_ref, m_ref, wi_ref, cw_ref, wo_ref, g_ref, b_ref,
                 o_ref):
    tm = h_ref.shape[0]
    n = tm + 2 * HALO
    i = pl.program_id(0)
    first = (i % tiles_per_seq) == 0
    last = (i % tiles_per_seq) == tiles_per_seq - 1
    shift, scale, gate = _mod_rows(m_ref, 1)
    x = h_ref[...]
    xf = jnp.concatenate([hp_ref[...], x, hn_ref[...]], axis=0)
    z = (xf * (1.0 + scale) + shift).astype(BF16)
    p = jnp.dot(z, wi_ref[...], preferred_element_type=F32)
    u = p[:, SC_WIDTH:2 * SC_WIDTH] * p[:, 2 * SC_WIDTH:]
    t = _row_index(n)
    outside = ((t < HALO) & first) | ((t >= tm + HALO) & last)
    u = jnp.where(outside, 0.0, u)
    cv = (cw_ref[0:1, :] * pltpu.roll(u, 1, 0) + cw_ref[1:2, :] * u
          + cw_ref[2:3, :] * pltpu.roll(u, n - 1, 0))
    yv = (p[HALO:HALO + tm, :SC_WIDTH] * cv[HALO:HALO + tm, :]).astype(BF16)
    y = jnp.dot(yv, wo_ref[...], preferred_element_type=F32)
    o_ref[...] = _post_norm(x, y, gate, 1.0, g_ref[...], b_ref[...])


def _mix1(h, m, row_of_tile, w_in, conv_w, w_out, g, b):
    n = h.shape[0]
    tm = TOKEN_TILE
    halo_per_tile = tm // HALO
    n_halo_blocks = n // HALO
    return pl.pallas_call(
        functools.partial(_mix1_kernel, SEQ // tm),
        out_shape=jax.ShapeDtypeStruct((n, D_MODEL), F32),
        grid=(n // tm,),
        in_specs=[
            pl.BlockSpec((HALO, D_MODEL), lambda i: (jnp.maximum(i * halo_per_tile - 1, 0), 0)),
            pl.BlockSpec((tm, D_MODEL), lambda i: (i, 0)),
            pl.BlockSpec((HALO, D_MODEL),
                         lambda i: (jnp.minimum((i + 1) * halo_per_tile, n_halo_blocks - 1), 0)),
            pl.BlockSpec((1, N_MOD, D_MODEL), lambda i: (row_of_tile(i), 0, 0)),
            _resident((D_MODEL, 3 * SC_WIDTH)),
            _resident((SC_K, SC_WIDTH)),
            _resident((SC_WIDTH, D_MODEL)),
            _resident((1, D_MODEL)),
            _resident((1, D_MODEL)),
        ],
        out_specs=pl.BlockSpec((tm, D_MODEL), lambda i: (i, 0)),
        compiler_params=_params(1),
        name="mix1_short_conv",
    )(h, h, h, m, w_in, conv_w, w_out, g, b)


def kernel(x, c, ctx, c_ctx, mod_w, mod_b, ln_g, ln_b, ffn_w1, ffn_w3, ffn_w2, mix0_w_in, na_rpb,
           lru_conv_w, lru_conv_b, lru_w_a, lru_b_a, lru_w_x, lru_b_x, lru_lambda, mix0_w_out,
           mix1_w_in, sconv_w, mix1_w_out):
    assert x.shape == (BATCH, SEQ, D_MODEL) and ctx.shape == (BATCH, CTX_LEN, D_MODEL)
    assert SEQ % TOKEN_TILE == 0 and (BATCH * CTX_LEN) % TOKEN_TILE == 0
    h = x.reshape(BATCH * SEQ, D_MODEL)
    hc = ctx.reshape(BATCH * CTX_LEN, D_MODEL)
    m_all = _modulation(c, c_ctx, mod_w, mod_b)

    tiles_per_seq = SEQ // TOKEN_TILE

    def sample_row(i):
        return i // tiles_per_seq

    def ctx_row(i):
        return CTX_ROW

    def ffn_weights(layer, idx):
        return (ffn_w1[layer, idx].astype(BF16), ffn_w3[layer, idx].astype(BF16),
                ffn_w2[layer, idx].astype(BF16))

    def ln(layer, sub):
        return ln_g[layer, sub][None, :], ln_b[layer, sub][None, :]

    m = m_all[0]
    w1, w3, w2 = ffn_weights(0, 0)
    g, b = ln(0, 0)
    h = _ffn(h, m, sample_row, w1, w3, w2, g, b, 0)
    hc = _ffn(hc, m, ctx_row, w1, w3, w2, g, b, 0)

    w_in = mix0_w_in[0].astype(BF16)
    q_scale = NA_HEAD_DIM ** -0.5
    lat_seg = SEQ // LRU_SEGMENTS
    ctx_seg = CTX_LEN // LRU_SEGMENTS
    q, k, v, xr, gr = _proj(h, m, sample_row, w_in, [
        (0, NA_WIDTH, BF16, q_scale, None),
        (NA_WIDTH, NA_WIDTH, BF16, 1.0, None),
        (2 * NA_WIDTH, NA_WIDTH, BF16, 1.0, None),
        (3 * NA_WIDTH, LRU_WIDTH, F32, 1.0, lat_seg),
        (3 * NA_WIDTH + LRU_WIDTH, LRU_WIDTH, F32, 1.0, lat_seg),
    ], 1)
    kc, vc, xrc = _proj(hc, m, ctx_row, w_in[:, NA_WIDTH:3 * NA_WIDTH + LRU_WIDTH], [
        (0, NA_WIDTH, BF16, 1.0, None),
        (NA_WIDTH, NA_WIDTH, BF16, 1.0, None),
        (2 * NA_WIDTH, LRU_WIDTH, F32, 1.0, ctx_seg),
    ], 1)
    yg_slabs = _lru(xrc, xr, gr, lru_conv_w[0], lru_conv_b[0], lru_w_a[0], lru_b_a[0], lru_w_x[0],
                    lru_b_x[0], lru_lambda[0])
    na = _neighbourhood_attention(q, k, v, kc, vc, na_rpb[0])
    g, b = ln(0, 1)
    h = _mix0_out(na, yg_slabs, h, m, sample_row, mix0_w_out[0].astype(BF16), g, b)
    w1, w3, w2 = ffn_weights(0, 1)
    g, b = ln(0, 2)
    h = _ffn(h, m, sample_row, w1, w3, w2, g, b, 2)

    m = m_all[1]
    w1, w3, w2 = ffn_weights(1, 0)
    g, b = ln(1, 0)
    h = _ffn(h, m, sample_row, w1, w3, w2, g, b, 0)
    g, b = ln(1, 1)
    h = _mix1(h, m, sample_row, mix1_w_in[0].astype(BF16), sconv_w[0],
              mix1_w_out[0].astype(BF16), g, b)
    w1, w3, w2 = ffn_weights(1, 1)
    g, b = ln(1, 2)
    h = _ffn(h, m, sample_row, w1, w3, w2, g, b, 2)
    return h.reshape(BATCH, SEQ, D_MODEL)
```

```python
import functools

import numpy as np
import jax
import jax.numpy as jnp
from jax import lax
from jax.experimental import pallas as pl
from jax.experimental.pallas import tpu as pltpu

F32 = jnp.float32
BF16 = jnp.bfloat16

D_MODEL = 1024
BATCH = 16
SEQ = 2048
DEPTH = 2
GRID_W = 64
CTX_LEN = 256

NA_HEADS = 8
NA_HEAD_DIM = 64
NA_WIDTH = NA_HEADS * NA_HEAD_DIM
NA_KH = 8
NA_KW = 16
LRU_WIDTH = 512
LRU_BLOCKS = 8
LRU_BLOCK = LRU_WIDTH // LRU_BLOCKS
LRU_CONV = 4
LRU_CONV_LEFT = 2
LRU_C = 8.0
LN_2 = 0.6931471805599453
SC_WIDTH = D_MODEL
SC_K = 3
D_FF = 2816
FFN_RES = 0.5
N_SUB = 3
N_MOD = 3 * N_SUB
ALPHA = (2.0 * DEPTH) ** 0.25
LN_EPS = 1e-5
NEG_INF = -1e30

ROWS = SEQ // GRID_W

SUBLANES = 8
LANES = 128
VMEM_LIMIT_BYTES = 56 * 1024 * 1024

MOD_ROWS = 24
CTX_ROW = BATCH
TOKEN_TILE = 512
NA_GROUP_ROWS = 4
NA_KEY_ROWS = 12
NA_GROUPS = ROWS // NA_GROUP_ROWS
LRU_SEGMENTS = SUBLANES
LRU_CHUNK = 256
HALO = SUBLANES


def _params(n_axes):
    return pltpu.CompilerParams(dimension_semantics=("arbitrary",) * n_axes,
                                vmem_limit_bytes=VMEM_LIMIT_BYTES)


def _resident(shape):
    return pl.BlockSpec(shape, lambda *_: (0,) * len(shape), pipeline_mode=pl.Buffered(1))


def _segment_pitch(length):
    return length if (length // SUBLANES) % 2 == 1 else length + SUBLANES


def _sigmoid(x):
    return 1.0 / (1.0 + jnp.exp(-x))


def _mod_rows(m_ref, sub):
    shift = m_ref[0, 3 * sub:3 * sub + 1, :]
    scale = m_ref[0, 3 * sub + 1:3 * sub + 2, :]
    gate = m_ref[0, 3 * sub + 2:3 * sub + 3, :]
    return shift, scale, gate


def _post_norm(x, y, gate, res_w, g, b):
    r = ALPHA * x + (res_w * gate) * y
    mu = jnp.mean(r, axis=-1, keepdims=True)
    rc = r - mu
    var = jnp.mean(rc * rc, axis=-1, keepdims=True)
    return rc * lax.rsqrt(var + LN_EPS) * g + b


def _mod_kernel(c_ref, w_ref, b_ref, o_ref):
    cv = c_ref[...]
    cond = (cv * _sigmoid(cv)).astype(BF16)
    o_ref[...] = jnp.dot(cond, w_ref[...].astype(BF16), preferred_element_type=F32) + b_ref[...]


def _modulation(c, c_ctx, mod_w, mod_b):
    pad = jnp.zeros((MOD_ROWS - BATCH - 1, D_MODEL), F32)
    cond = jnp.concatenate([c, c_ctx[None, :], pad], axis=0)
    tn = D_MODEL
    out = pl.pallas_call(
        _mod_kernel,
        out_shape=jax.ShapeDtypeStruct((DEPTH, MOD_ROWS, N_MOD * D_MODEL), F32),
        grid=(DEPTH, N_MOD * D_MODEL // tn),
        in_specs=[
            pl.BlockSpec((MOD_ROWS, D_MODEL), lambda l, j: (0, 0)),
            pl.BlockSpec((None, D_MODEL, tn), lambda l, j: (l, 0, j)),
            pl.BlockSpec((None, 1, tn), lambda l, j: (l, 0, j)),
        ],
        out_specs=pl.BlockSpec((None, MOD_ROWS, tn), lambda l, j: (l, 0, j)),
        compiler_params=_params(2),
        name="modulation",
    )(cond, mod_w, mod_b.reshape(DEPTH, 1, N_MOD * D_MODEL))
    return out.reshape(DEPTH, MOD_ROWS, N_MOD, D_MODEL)


def _ffn_tile(x, m_ref, w1_ref, w3_ref, w2_ref, lng_ref, lnb_ref, layer, sub):
    shift, scale, gate = _mod_rows(m_ref, sub)
    z = (x * (1.0 + scale) + shift).astype(BF16)
    a = jnp.dot(z, w1_ref[...], preferred_element_type=F32)
    bb = jnp.dot(z, w3_ref[...], preferred_element_type=F32)
    u = (a * _sigmoid(a) * bb).astype(BF16)
    y = jnp.dot(u, w2_ref[...], preferred_element_type=F32)
    return _post_norm(x, y, gate, FFN_RES, lng_ref[layer, sub:sub + 1, :], lnb_ref[layer, sub:sub + 1, :])


def _ffn_kernel(layer, sub, h_ref, m_ref, w1_ref, w3_ref, w2_ref, lng_ref, lnb_ref, o_ref):
    o_ref[...] = _ffn_tile(h_ref[...], m_ref, w1_ref, w3_ref, w2_ref, lng_ref, lnb_ref, layer, sub)


def _mix0_ffn_kernel(layer, seg_len, *refs):
    n_slab = LRU_WIDTH // LANES
    na_ref = refs[0]
    yg_slabs = refs[1:1 + n_slab]
    h_ref, m_ref, wa_ref, wb_ref, w1_ref, w3_ref, w2_ref, lng_ref, lnb_ref, o_ref = refs[1 + n_slab:]
    tm = h_ref.shape[0]
    pitch = _segment_pitch(seg_len)
    _, _, gate = _mod_rows(m_ref, 1)
    yg = jnp.concatenate([
        jnp.concatenate([slab[j * pitch:j * pitch + seg_len, :] for slab in yg_slabs], axis=1)
        for j in range(tm // seg_len)], axis=0).astype(BF16)
    y = (jnp.dot(na_ref[...], wa_ref[...], preferred_element_type=F32)
         + jnp.dot(yg, wb_ref[...], preferred_element_type=F32))
    x = _post_norm(h_ref[...], y, gate, 1.0, lng_ref[layer, 1:2, :], lnb_ref[layer, 1:2, :])
    o_ref[...] = _ffn_tile(x, m_ref, w1_ref, w3_ref, w2_ref, lng_ref, lnb_ref, layer, 2)


def _mod_spec(layer, row_of_tile):
    return pl.BlockSpec((None, 1, N_MOD, D_MODEL), lambda i: (layer, row_of_tile(i), 0, 0))


def _ffn_weight_specs(layer, idx):
    def spec(shape):
        return pl.BlockSpec((None, None) + shape, lambda i: (layer, idx, 0, 0),
                            pipeline_mode=pl.Buffered(1))
    return [spec((D_MODEL, D_FF)), spec((D_MODEL, D_FF)), spec((D_FF, D_MODEL)),
            _resident((DEPTH, N_SUB, D_MODEL)), _resident((DEPTH, N_SUB, D_MODEL))]


def _ffn(h, m_all, row_of_tile, ffn_w, ln_g, ln_b, layer, idx):
    n = h.shape[0]
    tm = TOKEN_TILE
    return pl.pallas_call(
        functools.partial(_ffn_kernel, layer, 2 * idx),
        out_shape=jax.ShapeDtypeStruct((n, D_MODEL), F32),
        grid=(n // tm,),
        in_specs=[pl.BlockSpec((tm, D_MODEL), lambda i: (i, 0)), _mod_spec(layer, row_of_tile)]
        + _ffn_weight_specs(layer, idx),
        out_specs=pl.BlockSpec((tm, D_MODEL), lambda i: (i, 0)),
        compiler_params=_params(1),
        name="ffn",
    )(h, m_all, *ffn_w, ln_g, ln_b)


def _mix0_ffn(na, yg_slabs, h, m_all, row_of_tile, w_out, ffn_w, ln_g, ln_b, layer):
    n = h.shape[0]
    tm = TOKEN_TILE
    seg_len = SEQ // LRU_SEGMENTS
    slab_rows = tm // seg_len * _segment_pitch(seg_len)
    return pl.pallas_call(
        functools.partial(_mix0_ffn_kernel, layer, seg_len),
        out_shape=jax.ShapeDtypeStruct((n, D_MODEL), F32),
        grid=(n // tm,),
        in_specs=[pl.BlockSpec((tm, NA_WIDTH), lambda i: (i, 0))]
        + [pl.BlockSpec((slab_rows, LANES), lambda i: (i, 0))] * len(yg_slabs)
        + [pl.BlockSpec((tm, D_MODEL), lambda i: (i, 0)), _mod_spec(layer, row_of_tile),
           _resident((NA_WIDTH, D_MODEL)), _resident((LRU_WIDTH, D_MODEL))]
        + _ffn_weight_specs(layer, 1),
        out_specs=pl.BlockSpec((tm, D_MODEL), lambda i: (i, 0)),
        compiler_params=_params(1),
        name="mix0_out_ffn",
    )(na, *yg_slabs, h, m_all, w_out[:NA_WIDTH], w_out[NA_WIDTH:], *ffn_w, ln_g, ln_b)


def _proj_kernel(sub, outs, h_ref, m_ref, w_ref, *o_refs):
    shift, scale, _ = _mod_rows(m_ref, sub)
    z = (h_ref[...] * (1.0 + scale) + shift).astype(BF16)
    p = jnp.dot(z, w_ref[...], preferred_element_type=F32)
    tm = h_ref.shape[0]
    for (start, width, _, mult, seg_len), o_ref in zip(outs, o_refs):
        v = p[:, start:start + width]
        if mult != 1.0:
            v = v * mult
        v = v.astype(o_ref.dtype)
        if seg_len is None:
            o_ref[...] = v
            continue
        pitch = _segment_pitch(seg_len)
        for j in range(tm // seg_len):
            o_ref[j * pitch:j * pitch + seg_len, :] = v[j * seg_len:(j + 1) * seg_len, :]
            if pitch > seg_len:
                o_ref[j * pitch + seg_len:(j + 1) * pitch, :] = jnp.zeros(
                    (pitch - seg_len, width), o_ref.dtype)


def _proj(h, m_all, row_of_tile, w, outs, layer, sub):
    n = h.shape[0]
    tm = TOKEN_TILE

    def out_rows(rows, seg_len):
        return rows if seg_len is None else rows // seg_len * _segment_pitch(seg_len)

    return pl.pallas_call(
        functools.partial(_proj_kernel, sub, outs),
        out_shape=[jax.ShapeDtypeStruct((out_rows(n, sl), width), dt) for _, width, dt, _, sl in outs],
        grid=(n // tm,),
        in_specs=[
            pl.BlockSpec((tm, D_MODEL), lambda i: (i, 0)),
            _mod_spec(layer, row_of_tile),
            _resident(w.shape),
        ],
        out_specs=[pl.BlockSpec((out_rows(tm, sl), width), lambda i: (i, 0))
                   for _, width, _, _, sl in outs],
        compiler_params=_params(1),
        name="mix0_in_proj",
    )(h, m_all, w)


def _row_index(n):
    return lax.broadcasted_iota(jnp.int32, (n, 1), 0)


def _load_segment_rows(slabs, v, pitch):
    return jnp.concatenate([slab[pl.ds(v, LRU_SEGMENTS, stride=pitch), :] for slab in slabs], axis=1)


def _lru_coeffs(slabs, length, cw_ref, cb_ref, wg_ref, bg_ref, rate, xs, a_s, b_s):
    pitch = _segment_pitch(length)
    rows = LRU_SEGMENTS * length
    lead = LRU_CONV_LEFT * LRU_SEGMENTS
    for v in range(length):
        xs[lead + v * LRU_SEGMENTS:lead + (v + 1) * LRU_SEGMENTS, :] = _load_segment_rows(slabs, v, pitch)
    seg = _row_index(LRU_SEGMENTS)
    for v in range(-LRU_CONV_LEFT, 0):
        src = xs[lead + (length + v) * LRU_SEGMENTS:lead + (length + v + 1) * LRU_SEGMENTS, :]
        xs[lead + v * LRU_SEGMENTS:lead + (v + 1) * LRU_SEGMENTS, :] = jnp.where(
            seg >= 1, pltpu.roll(src, 1, 0), 0.0)
    for v in range(LRU_CONV - 1 - LRU_CONV_LEFT):
        src = xs[lead + v * LRU_SEGMENTS:lead + (v + 1) * LRU_SEGMENTS, :]
        xs[lead + rows + v * LRU_SEGMENTS:lead + rows + (v + 1) * LRU_SEGMENTS, :] = jnp.where(
            seg < LRU_SEGMENTS - 1, pltpu.roll(src, LRU_SEGMENTS - 1, 0), 0.0)

    chunk = min(rows, LRU_CHUNK)
    for c0 in range(0, rows, chunk):
        xc = cb_ref[...]
        for k in range(LRU_CONV):
            off = c0 + k * LRU_SEGMENTS
            xc = xc + cw_ref[k:k + 1, :] * xs[off:off + chunk, :]
        th = jnp.tanh(jnp.dot(xc.astype(BF16), wg_ref[...], preferred_element_type=F32) + bg_ref[...])
        xh = 0.5 * xc
        for d in range(2):
            base = 2 * LRU_WIDTH * d
            th_r = th[:, base:base + LRU_WIDTH]
            th_i = th[:, base + LRU_WIDTH:base + 2 * LRU_WIDTH]
            log2_a = th_r * rate[d:d + 1, :] + rate[d:d + 1, :]
            a = jnp.exp2(log2_a)
            a_s[d, c0:c0 + chunk, :] = a
            var = jnp.tanh(log2_a * (-LN_2)) * (a * a + 1.0)
            root = jnp.where(var > 0.0, var * lax.rsqrt(var), 0.0)
            b_s[d, c0:c0 + chunk, :] = root * (th_i * xh + xh)


def _lru_scan(length, a_s, b_s):
    zeros = jnp.zeros((LRU_SEGMENTS, LRU_WIDTH), F32)
    ones = jnp.ones((LRU_SEGMENTS, LRU_WIDTH), F32)

    def body(j, carry):
        hf, pf, hb, pb = carry
        rf = pl.ds(pl.multiple_of(j * LRU_SEGMENTS, LRU_SEGMENTS), LRU_SEGMENTS)
        af = a_s[0, rf, :]
        hf = af * hf + b_s[0, rf, :]
        pf = af * pf
        b_s[0, rf, :] = hf
        a_s[0, rf, :] = pf
        rb = pl.ds(pl.multiple_of((length - 1 - j) * LRU_SEGMENTS, LRU_SEGMENTS), LRU_SEGMENTS)
        ab = a_s[1, rb, :]
        hb = ab * hb + b_s[1, rb, :]
        pb = ab * pb
        b_s[1, rb, :] = hb
        a_s[1, rb, :] = pb
        return hf, pf, hb, pb

    return lax.fori_loop(0, length, body, (zeros, ones, zeros, ones), unroll=4)


def _segment_carries(h_end, p_end, h_in, order):
    seg = _row_index(LRU_SEGMENTS)
    carries = jnp.zeros((LRU_SEGMENTS, LRU_WIDTH), F32)
    for s in order:
        carries = jnp.where(seg == s, h_in, carries)
        h_in = h_end[s:s + 1, :] + p_end[s:s + 1, :] * h_in
    return carries, h_in


def _gelu_tanh(x):
    return 0.5 * x * (1.0 + jnp.tanh(0.7978845608028654 * (x + 0.044715 * (x * x * x))))


def _lru_kernel(*refs):
    n_slab = LRU_WIDTH // LANES
    xc_slabs, xl_slabs, gl_slabs = refs[0:n_slab], refs[n_slab:2 * n_slab], refs[2 * n_slab:3 * n_slab]
    cw_ref, cb_ref, wg_ref, bg_ref, lam_ref = refs[3 * n_slab:3 * n_slab + 5]
    o_slabs = refs[3 * n_slab + 5:4 * n_slab + 5]
    xs, a_s, b_s = refs[4 * n_slab + 5:]
    lam = lam_ref[...]
    log_sig = jnp.minimum(lam, 0.0) - jnp.log1p(jnp.exp(-jnp.abs(lam)))
    rate = (0.5 * LRU_C / LN_2) * log_sig
    zero = jnp.zeros((1, LRU_WIDTH), F32)
    fwd_order = range(LRU_SEGMENTS)
    bwd_order = range(LRU_SEGMENTS - 1, -1, -1)

    ctx_len = CTX_LEN // LRU_SEGMENTS
    _lru_coeffs(xc_slabs, ctx_len, cw_ref, cb_ref, wg_ref, bg_ref, rate, xs, a_s, b_s)
    hf, pf, hb, pb = _lru_scan(ctx_len, a_s, b_s)
    _, hf0 = _segment_carries(hf, pf, zero, fwd_order)
    _, hb0 = _segment_carries(hb, pb, zero, bwd_order)

    length = SEQ // LRU_SEGMENTS
    pitch = _segment_pitch(length)
    _lru_coeffs(xl_slabs, length, cw_ref, cb_ref, wg_ref, bg_ref, rate, xs, a_s, b_s)
    hf, pf, hb, pb = _lru_scan(length, a_s, b_s)
    cf, _ = _segment_carries(hf, pf, hf0, fwd_order)
    cb, _ = _segment_carries(hb, pb, hb0, bwd_order)

    rows = LRU_SEGMENTS * length
    chunk = min(rows, LRU_CHUNK)
    nv = chunk // LRU_SEGMENTS

    def carried(prod, carry):
        return (prod.reshape(nv, LRU_SEGMENTS, LRU_WIDTH) * carry[None]).reshape(chunk, LRU_WIDTH)

    for c0 in range(0, rows, chunk):
        y = (b_s[0, c0:c0 + chunk, :] + carried(a_s[0, c0:c0 + chunk, :], cf)
             + b_s[1, c0:c0 + chunk, :] + carried(a_s[1, c0:c0 + chunk, :], cb))
        v0 = c0 // LRU_SEGMENTS
        gate = jnp.concatenate([_load_segment_rows(gl_slabs, v0 + i, pitch) for i in range(nv)], axis=0)
        out = y * _gelu_tanh(gate)
        for i in range(nv):
            for cg, o_ref in enumerate(o_slabs):
                o_ref[pl.ds(v0 + i, LRU_SEGMENTS, stride=pitch), :] = out[
                    i * LRU_SEGMENTS:(i + 1) * LRU_SEGMENTS, cg * LANES:(cg + 1) * LANES]
    if pitch > length:
        for o_ref in o_slabs:
            for s in range(LRU_SEGMENTS):
                o_ref[s * pitch + length:(s + 1) * pitch, :] = jnp.zeros((pitch - length, LANES), F32)


def _block_diag(w):
    eye = jnp.eye(LRU_BLOCKS, dtype=w.dtype)
    return jnp.einsum('nkj,nm->nkmj', w, eye).reshape(LRU_WIDTH, LRU_WIDTH)


def _lru(xr_ctx, xr_lat, g_lat, conv_w, conv_b, w_a, b_a, w_x, b_x, lam):
    wg = (0.5 * jnp.concatenate([_block_diag(w_a[0]), _block_diag(w_x[0]),
                                 _block_diag(w_a[1]), _block_diag(w_x[1])], axis=1)).astype(BF16)
    bg = 0.5 * jnp.concatenate([b_a[0], b_x[0], b_a[1], b_x[1]])[None, :]
    n_slab = LRU_WIDTH // LANES
    rows_c = LRU_SEGMENTS * _segment_pitch(CTX_LEN // LRU_SEGMENTS)
    rows_l = LRU_SEGMENTS * _segment_pitch(SEQ // LRU_SEGMENTS)

    def slab_specs(rows):
        return [pl.BlockSpec((rows, LANES), functools.partial(lambda cg, i: (i, cg), cg))
                for cg in range(n_slab)]

    scan_rows = SEQ
    return pl.pallas_call(
        _lru_kernel,
        out_shape=[jax.ShapeDtypeStruct((BATCH * rows_l, LANES), F32)] * n_slab,
        grid=(BATCH,),
        in_specs=slab_specs(rows_c) + slab_specs(rows_l) + slab_specs(rows_l) + [
            _resident((LRU_CONV, LRU_WIDTH)),
            _resident((1, LRU_WIDTH)),
            _resident((LRU_WIDTH, 4 * LRU_WIDTH)),
            _resident((1, 4 * LRU_WIDTH)),
            _resident((2, LRU_WIDTH)),
        ],
        out_specs=[pl.BlockSpec((rows_l, LANES), lambda i: (i, 0))] * n_slab,
        scratch_shapes=[
            pltpu.VMEM((scan_rows + (LRU_CONV - 1) * LRU_SEGMENTS, LRU_WIDTH), F32),
            pltpu.VMEM((2, scan_rows, LRU_WIDTH), F32),
            pltpu.VMEM((2, scan_rows, LRU_WIDTH), F32),
        ],
        compiler_params=_params(1),
        name="rglru",
    )(*([xr_ctx] * n_slab + [xr_lat] * n_slab + [g_lat] * n_slab),
      conv_w, conv_b[None, :], wg, bg, lam)


def _na_key_row_start(g):
    return int(np.clip(NA_GROUP_ROWS * g - NA_KH // 2, 0, ROWS - NA_KEY_ROWS))


def _na_variant(g):
    return 0 if g == 0 else (2 if g == NA_GROUPS - 1 else 1)


def _na_block_tables():
    masked_slot = 2 * NA_KH - 1
    tabs = {}
    for g in range(NA_GROUPS):
        r = (NA_GROUP_ROWS * g + np.arange(NA_GROUP_ROWS))[:, None]
        kr = (_na_key_row_start(g) + np.arange(NA_KEY_ROWS))[None, :]
        r0 = np.clip(r - NA_KH // 2, 0, ROWS - NA_KH)
        in_window = (kr >= r0) & (kr < r0 + NA_KH)
        tab = np.where(in_window, kr - r + NA_KH - 1, masked_slot)
        v = _na_variant(g)
        assert v not in tabs or np.array_equal(tabs[v], tab)
        tabs[v] = tab
    return np.stack([tabs[v] for v in sorted(tabs)])


def _na_bias(rpb):
    n_rel = 2 * NA_KH - 1
    period = 2 * GRID_W - 1
    lead = GRID_W - NA_KW
    padded = jnp.pad(rpb.astype(F32), ((0, 0), (0, 0), (lead, period - lead - (2 * NA_KW - 1))))
    tiled = jnp.tile(padded, (1, 1, GRID_W + 1))[..., :GRID_W * (period + 1)]
    toeplitz = tiled.reshape(NA_HEADS, n_rel, GRID_W, period + 1)[..., ::-1, :GRID_W]
    cq = np.arange(GRID_W)[:, None]
    kc = np.arange(GRID_W)[None, :]
    start = np.clip(cq - NA_KW // 2, 0, GRID_W - NA_KW)
    col_ok = (kc >= start) & (kc < start + NA_KW)
    blocks = jnp.where(col_ok, toeplitz, NEG_INF)
    blocks = jnp.concatenate([blocks, jnp.full((NA_HEADS, 1, GRID_W, GRID_W), NEG_INF, F32)], axis=1)
    variants = []
    for tab in _na_block_tables():
        rows = [jnp.concatenate([blocks[:, int(s)] for s in tab_row], axis=-1) for tab_row in tab]
        variants.append(jnp.concatenate(rows, axis=-2))
    return jnp.stack(variants, axis=1)


def _na_kernel(q_ref, k_ref, v_ref, kc_ref, vc_ref, bias_ref, o_ref):
    gq = NA_GROUP_ROWS * GRID_W
    gk = NA_KEY_ROWS * GRID_W
    lane = lax.broadcasted_iota(jnp.int32, (1, LANES), 1)
    kc = kc_ref[...]
    vc = vc_ref[...]
    contract_last = (((1,), (1,)), ((), ()))
    for g in range(NA_GROUPS):
        k0 = _na_key_row_start(g) * GRID_W
        q = q_ref[g * gq:(g + 1) * gq, :]
        kw = k_ref[k0:k0 + gk, :]
        vw = v_ref[k0:k0 + gk, :]
        outs = []
        for hh in range(LANES // NA_HEAD_DIM):
            head_lanes = (lane >= hh * NA_HEAD_DIM) & (lane < (hh + 1) * NA_HEAD_DIM)
            qh = jnp.where(head_lanes, q, jnp.zeros_like(q))
            s_loc = lax.dot_general(qh, kw, contract_last, preferred_element_type=F32)
            s_loc = s_loc + bias_ref[hh, _na_variant(g)]
            s_ctx = lax.dot_general(qh, kc, contract_last, preferred_element_type=F32)
            mx = jnp.maximum(jnp.max(s_loc, axis=-1, keepdims=True),
                             jnp.max(s_ctx, axis=-1, keepdims=True))
            p_loc = jnp.exp(s_loc - mx)
            p_ctx = jnp.exp(s_ctx - mx)
            den = jnp.sum(p_loc, axis=-1, keepdims=True) + jnp.sum(p_ctx, axis=-1, keepdims=True)
            o = (jnp.dot(p_loc.astype(BF16), vw, preferred_element_type=F32)
                 + jnp.dot(p_ctx.astype(BF16), vc, preferred_element_type=F32))
            outs.append((head_lanes, o / den))
        res = outs[0][1]
        for head_lanes, o in outs[1:]:
            res = jnp.where(head_lanes, o, res)
        o_ref[g * gq:(g + 1) * gq, :] = res.astype(BF16)


def _neighbourhood_attention(q, k, v, kc, vc, rpb):
    bias = _na_bias(rpb)
    heads_per_step = LANES // NA_HEAD_DIM
    n_pairs = NA_HEADS // heads_per_step
    gq = NA_GROUP_ROWS * GRID_W
    gk = NA_KEY_ROWS * GRID_W
    return pl.pallas_call(
        _na_kernel,
        out_shape=jax.ShapeDtypeStruct((BATCH * SEQ, NA_WIDTH), BF16),
        grid=(n_pairs, BATCH),
        in_specs=[
            pl.BlockSpec((SEQ, LANES), lambda p, b: (b, p)),
            pl.BlockSpec((SEQ, LANES), lambda p, b: (b, p)),
            pl.BlockSpec((SEQ, LANES), lambda p, b: (b, p)),
            pl.BlockSpec((CTX_LEN, LANES), lambda p, b: (b, p)),
            pl.BlockSpec((CTX_LEN, LANES), lambda p, b: (b, p)),
            pl.BlockSpec((heads_per_step, 3, gq, gk), lambda p, b: (p, 0, 0, 0)),
        ],
        out_specs=pl.BlockSpec((SEQ, LANES), lambda p, b: (b, p)),
        compiler_params=_params(2),
        name="neighbourhood_attention",
    )(q, k, v, kc, vc, bias)


def _mix1_kernel(layer, tiles_per_seq, hp_ref, h_ref, hn_ref, m_ref, wi_ref, cw_ref, wo_ref,
                 lng_ref, lnb_ref, o_ref):
    tm = h_ref.shape[0]
    n = tm + 2 * HALO
    i = pl.program_id(0)
    first = (i % tiles_per_seq) == 0
    last = (i % tiles_per_seq) == tiles_per_seq - 1
    shift, scale, gate = _mod_rows(m_ref, 1)
    x = h_ref[...]
    xf = jnp.concatenate([hp_ref[...], x, hn_ref[...]], axis=0)
    z = (xf * (1.0 + scale) + shift).astype(BF16)
    p = jnp.dot(z, wi_ref[...], preferred_element_type=F32)
    u = p[:, SC_WIDTH:2 * SC_WIDTH] * p[:, 2 * SC_WIDTH:]
    t = _row_index(n)
    outside = ((t < HALO) & first) | ((t >= tm + HALO) & last)
    u = jnp.where(outside, 0.0, u)
    cv = (cw_ref[0:1, :] * pltpu.roll(u, 1, 0) + cw_ref[1:2, :] * u
          + cw_ref[2:3, :] * pltpu.roll(u, n - 1, 0))
    yv = (p[HALO:HALO + tm, :SC_WIDTH] * cv[HALO:HALO + tm, :]).astype(BF16)
    y = jnp.dot(yv, wo_ref[...], preferred_element_type=F32)
    o_ref[...] = _post_norm(x, y, gate, 1.0, lng_ref[layer, 1:2, :], lnb_ref[layer, 1:2, :])


def _mix1(h, m_all, row_of_tile, w_in, conv_w, w_out, ln_g, ln_b, layer):
    n = h.shape[0]
    tm = TOKEN_TILE
    halo_per_tile = tm // HALO
    n_halo_blocks = n // HALO
    return pl.pallas_call(
        functools.partial(_mix1_kernel, layer, SEQ // tm),
        out_shape=jax.ShapeDtypeStruct((n, D_MODEL), F32),
        grid=(n // tm,),
        in_specs=[
            pl.BlockSpec((HALO, D_MODEL), lambda i: (jnp.maximum(i * halo_per_tile - 1, 0), 0)),
            pl.BlockSpec((tm, D_MODEL), lambda i: (i, 0)),
            pl.BlockSpec((HALO, D_MODEL),
                         lambda i: (jnp.minimum((i + 1) * halo_per_tile, n_halo_blocks - 1), 0)),
            _mod_spec(layer, row_of_tile),
            _resident((D_MODEL, 3 * SC_WIDTH)),
            _resident((SC_K, SC_WIDTH)),
            _resident((SC_WIDTH, D_MODEL)),
            _resident((DEPTH, N_SUB, D_MODEL)),
            _resident((DEPTH, N_SUB, D_MODEL)),
        ],
        out_specs=pl.BlockSpec((tm, D_MODEL), lambda i: (i, 0)),
        compiler_params=_params(1),
        name="mix1_short_conv",
    )(h, h, h, m_all, w_in, conv_w, w_out, ln_g, ln_b)


def kernel(x, c, ctx, c_ctx, mod_w, mod_b, ln_g, ln_b, ffn_w1, ffn_w3, ffn_w2, mix0_w_in, na_rpb,
           lru_conv_w, lru_conv_b, lru_w_a, lru_b_a, lru_w_x, lru_b_x, lru_lambda, mix0_w_out,
           mix1_w_in, sconv_w, mix1_w_out):
    assert x.shape == (BATCH, SEQ, D_MODEL) and ctx.shape == (BATCH, CTX_LEN, D_MODEL)
    assert SEQ % TOKEN_TILE == 0 and (BATCH * CTX_LEN) % TOKEN_TILE == 0
    h = x.reshape(BATCH * SEQ, D_MODEL)
    hc = ctx.reshape(BATCH * CTX_LEN, D_MODEL)
    m_all = _modulation(c, c_ctx, mod_w, mod_b)

    tiles_per_seq = SEQ // TOKEN_TILE

    def sample_row(i):
        return i // tiles_per_seq

    def ctx_row(i):
        return CTX_ROW

    ffn_w = (ffn_w1.astype(BF16), ffn_w3.astype(BF16), ffn_w2.astype(BF16))

    h = _ffn(h, m_all, sample_row, ffn_w, ln_g, ln_b, 0, 0)
    hc = _ffn(hc, m_all, ctx_row, ffn_w, ln_g, ln_b, 0, 0)

    w_in = mix0_w_in[0].astype(BF16)
    q_scale = NA_HEAD_DIM ** -0.5
    lat_seg = SEQ // LRU_SEGMENTS
    ctx_seg = CTX_LEN // LRU_SEGMENTS
    q, k, v, xr, gr = _proj(h, m_all, sample_row, w_in, [
        (0, NA_WIDTH, BF16, q_scale, None),
        (NA_WIDTH, NA_WIDTH, BF16, 1.0, None),
        (2 * NA_WIDTH, NA_WIDTH, BF16, 1.0, None),
        (3 * NA_WIDTH, LRU_WIDTH, F32, 1.0, lat_seg),
        (3 * NA_WIDTH + LRU_WIDTH, LRU_WIDTH, F32, 1.0, lat_seg),
    ], 0, 1)
    kc, vc, xrc = _proj(hc, m_all, ctx_row, w_in[:, NA_WIDTH:3 * NA_WIDTH + LRU_WIDTH], [
        (0, NA_WIDTH, BF16, 1.0, None),
        (NA_WIDTH, NA_WIDTH, BF16, 1.0, None),
        (2 * NA_WIDTH, LRU_WIDTH, F32, 1.0, ctx_seg),
    ], 0, 1)
    yg_slabs = _lru(xrc, xr, gr, lru_conv_w[0], lru_conv_b[0], lru_w_a[0], lru_b_a[0], lru_w_x[0],
                    lru_b_x[0], lru_lambda[0])
    na = _neighbourhood_attention(q, k, v, kc, vc, na_rpb[0])
    h = _mix0_ffn(na, yg_slabs, h, m_all, sample_row, mix0_w_out[0].astype(BF16), ffn_w, ln_g, ln_b, 0)

    h = _ffn(h, m_all, sample_row, ffn_w, ln_g, ln_b, 1, 0)
    h = _mix1(h, m_all, sample_row, mix1_w_in[0].astype(BF16), sconv_w[0],
              mix1_w_out[0].astype(BF16), ln_g, ln_b, 1)
    h = _ffn(h, m_all, sample_row, ffn_w, ln_g, ln_b, 1, 1)
    return h.reshape(BATCH, SEQ, D_MODEL)
```
